```python
import math
import jax, jax.numpy as jnp
from jax import lax
import numpy as np

D_MODEL = 1024
BATCH = 8
SEQ = 4096
DEPTH = 2

GRID_W = 64
CTX_LEN = 256
HEAD_DIM = 64
HA = 4
DA = HEAD_DIM
HB = 8
KVB = 2
GB = HB // KVB
DH = HEAD_DIM
WINDOW = 128
BLOCK = 128
ROPE_BASE = 10000.0
N_EXPERTS = 16
N_GROUPS = 4
EXPERTS_PER_GROUP = N_EXPERTS // N_GROUPS
TOP_K = 2
D_EXPERT = 512
MOE_CHUNK = 128
EPS = 1e-6
NEG = -1e30

D_QA = HA * 2 * DA
D_VA = HA * 2 * DA
D_QB = HB * DH
D_KB = KVB * DH
D_IN = 2 * D_QA + D_VA + D_QB + 2 * D_KB
D_MIX_OUT = HA * 2 * DA + HB * DH
SPLITS = (D_QA, 2 * D_QA, 2 * D_QA + D_VA, 2 * D_QA + D_VA + D_QB, 2 * D_QA + D_VA + D_QB + D_KB)

kernel_name = "hybrid_diffattn_swa_moe_dit_block"


def rmsnorm(x, g):
    xf = x.astype(jnp.float32)
    y = xf * lax.rsqrt(jnp.mean(xf * xf, axis=-1, keepdims=True) + EPS)
    return (y * g.astype(jnp.float32)).astype(x.dtype)


def axial_rope_tables(n):
    rows = n // GRID_W
    row = jnp.broadcast_to(jnp.arange(rows)[:, None], (rows, GRID_W)).reshape(-1)
    col = jnp.broadcast_to(jnp.arange(GRID_W)[None, :], (rows, GRID_W)).reshape(-1)
    n_freq = HEAD_DIM // 4
    inv = 1.0 / (ROPE_BASE ** (jnp.arange(n_freq, dtype=jnp.float32) / n_freq))
    ang = jnp.stack([row, col], axis=-1).astype(jnp.float32)[:, :, None] * inv
    return jnp.cos(ang), jnp.sin(ang)


def apply_rope(x, cos, sin):
    xs = x.reshape(*x.shape[:-1], 2, 2, HEAD_DIM // 4)
    x1, x2 = xs[..., 0, :], xs[..., 1, :]
    cs = cos[:, None].astype(x.dtype)
    sn = sin[:, None].astype(x.dtype)
    out = jnp.stack([x1 * cs - x2 * sn, x2 * cs + x1 * sn], axis=-2)
    return out.reshape(x.shape)


def split_heads(p):
    B, n, _ = p.shape
    qa, ka, va, qb, kb, vb = jnp.split(p, list(SPLITS), axis=-1)
    qa = qa.reshape(B, n, HA, 2, DA)
    ka = ka.reshape(B, n, HA, 2, DA)
    return (qa[..., 0, :], qa[..., 1, :], ka[..., 0, :], ka[..., 1, :],
            va.reshape(B, n, HA, 2 * DA), qb.reshape(B, n, HB, DH),
            kb.reshape(B, n, KVB, DH), vb.reshape(B, n, KVB, DH))


def diff_mix(q1, q2, k1, k2, v, lam):
    scale = DA ** -0.5
    s1 = jnp.einsum('bqhd,bkhd->bhqk', q1, k1).astype(jnp.float32) * scale
    s2 = jnp.einsum('bqhd,bkhd->bhqk', q2, k2).astype(jnp.float32) * scale
    a = (jax.nn.softmax(s1, axis=-1) - lam * jax.nn.softmax(s2, axis=-1)).astype(v.dtype)
    return jnp.einsum('bhqk,bkhe->bqhe', a, v)


def sink_attend(q, k, v, bias, sink):
    B, Q = q.shape[0], q.shape[1]
    qg = q.reshape(B, Q, KVB, GB, DH)
    s = jnp.einsum('bqhgd,bkhd->bhgqk', qg, k).astype(jnp.float32) * (DH ** -0.5)
    if bias is not None:
        s = s + bias
    sk = jnp.broadcast_to(sink.astype(jnp.float32).reshape(1, KVB, GB, 1, 1), s.shape[:-1] + (1,))
    p = jax.nn.softmax(jnp.concatenate([s, sk], axis=-1), axis=-1)[..., :-1]
    o = jnp.einsum('bhgqk,bkhd->bqhgd', p.astype(v.dtype), v)
    return o.reshape(B, Q, HB * DH)


def latent_mixers(h, ctx_kv, w_in_l, lam, lam_init, subln_l, sink_l, cos, sin):
    B, n, _ = h.shape
    nblk = n // BLOCK
    cka1, cka2, cva, ckb, cvb = ctx_kv
    qa1, qa2, ka1, ka2, va, qb, kb, vb = split_heads(h @ w_in_l)
    qa1, qa2, ka1, ka2, qb, kb = (apply_rope(t, cos, sin) for t in (qa1, qa2, ka1, ka2, qb, kb))

    def to_blocks(t):
        return jnp.moveaxis(t.reshape(B, nblk, BLOCK, *t.shape[2:]), 1, 0)

    def from_blocks(t):
        return jnp.moveaxis(t, 0, 1).reshape(B, n, -1)

    k1 = jnp.concatenate([cka1, ka1], axis=1)
    k2 = jnp.concatenate([cka2, ka2], axis=1)
    v_a = jnp.concatenate([cva, va], axis=1)
    oa = lax.map(lambda qs: diff_mix(qs[0], qs[1], k1, k2, v_a, lam), (to_blocks(qa1), to_blocks(qa2)))
    oa = from_blocks(oa).reshape(B, n, HA, 2 * DA)
    oa = rmsnorm(oa, subln_l) * (1.0 - lam_init)

    def windows(t):
        tp = jnp.pad(t, ((0, 0), (BLOCK, BLOCK), (0, 0), (0, 0))).reshape(B, nblk + 2, BLOCK, KVB, DH)
        return jnp.moveaxis(jnp.concatenate([tp[:, :-2], tp[:, 1:-1], tp[:, 2:]], axis=2), 1, 0)

    q_off = jnp.arange(BLOCK)
    k_off = jnp.arange(3 * BLOCK) - BLOCK
    ctx_bias = jnp.zeros((BLOCK, ckb.shape[1]), jnp.float32)

    def win_block(args):
        qblk, kw, vw, i = args
        qpos = i * BLOCK + q_off
        kpos = i * BLOCK + k_off
        valid = ((jnp.abs(qpos[:, None] - kpos[None, :]) <= WINDOW)
                 & (kpos >= 0)[None, :] & (kpos < n)[None, :])
        bias = jnp.concatenate([jnp.where(valid, 0.0, NEG).astype(jnp.float32), ctx_bias], axis=-1)
        k = jnp.concatenate([kw, ckb], axis=1)
        v = jnp.concatenate([vw, cvb], axis=1)
        return sink_attend(qblk, k, v, bias, sink_l)

    ob = lax.map(win_block, (to_blocks(qb), windows(kb), windows(vb), jnp.arange(nblk)))
    ob = from_blocks(ob)
    return jnp.concatenate([oa.reshape(B, n, HA * 2 * DA), ob], axis=-1)


def route(h_flat, w_router, b_router):
    s = jax.nn.sigmoid((h_flat @ w_router).astype(jnp.float32))
    sb = s + b_router.astype(jnp.float32)
    gscore = lax.top_k(sb.reshape(-1, N_GROUPS, EXPERTS_PER_GROUP), TOP_K)[0].sum(-1)
    gsel = jnp.argmax(gscore, axis=-1)
    in_group = (jnp.arange(N_EXPERTS) // EXPERTS_PER_GROUP)[None, :] == gsel[:, None]
    _, idx = lax.top_k(jnp.where(in_group, sb, -jnp.inf), TOP_K)
    w = jnp.take_along_axis(s, idx, axis=-1)
    w = w / jnp.sum(w, axis=-1, keepdims=True)
    return jnp.sum(jax.nn.one_hot(idx, N_EXPERTS, dtype=jnp.float32) * w[..., None], axis=1)


def moe(h, w_router, b_router, w1_l, w3_l, w2_l):
    shp = h.shape
    hf = h.reshape(-1, D_MODEL)
    comb = route(hf, w_router, b_router).astype(h.dtype)

    def chunk(args):
        xc, gc = args
        a = jnp.einsum('td,edf->tef', xc, w1_l)
        b = jnp.einsum('td,edf->tef', xc, w3_l)
        u = jax.nn.silu(a) * b * gc[..., None]
        return jnp.einsum('tef,efd->td', u, w2_l)

    y = lax.map(chunk, (hf.reshape(-1, MOE_CHUNK, D_MODEL), comb.reshape(-1, MOE_CHUNK, N_EXPERTS)))
    return y.reshape(shp)


def setup_inputs(seed: int = 0) -> dict:
    key = jax.random.key(seed)
    ks = jax.random.split(key, 20)

    def nrm(k, shape, s):
        return jax.random.normal(k, shape, jnp.float32) * s

    return {
        "x": nrm(ks[0], (BATCH, SEQ, D_MODEL), 1.0),
        "c": nrm(ks[1], (BATCH, D_MODEL), 1.0),
        "ctx": nrm(ks[2], (BATCH, CTX_LEN, D_MODEL), 1.0),
        "c_ctx": nrm(ks[3], (D_MODEL,), 1.0),
        "w_ada": nrm(ks[4], (DEPTH, D_MODEL, 6 * D_MODEL), 0.5 * D_MODEL ** -0.5),
        "b_ada": nrm(ks[5], (DEPTH, 6 * D_MODEL), 0.02),
        "norm_attn": 1.0 + nrm(ks[6], (DEPTH, D_MODEL), 0.05),
        "norm_ffn": 1.0 + nrm(ks[7], (DEPTH, D_MODEL), 0.05),
        "w_in": nrm(ks[8], (DEPTH, D_MODEL, D_IN), D_MODEL ** -0.5),
        "w_out": nrm(ks[9], (DEPTH, D_MIX_OUT, D_MODEL), D_MIX_OUT ** -0.5),
        "lambda_qk": nrm(ks[10], (DEPTH, 4, DA), 0.1),
        "subln": 1.0 + nrm(ks[11], (DEPTH, 2 * DA), 0.05),
        "sink": nrm(ks[12], (DEPTH, HB), 0.5),
        "w_router": nrm(ks[13], (D_MODEL, N_EXPERTS), D_MODEL ** -0.5),
        "b_router": nrm(ks[14], (N_EXPERTS,), 0.01),
        "w1": nrm(ks[15], (DEPTH, N_EXPERTS, D_MODEL, D_EXPERT), D_MODEL ** -0.5),
        "w3": nrm(ks[16], (DEPTH, N_EXPERTS, D_MODEL, D_EXPERT), D_MODEL ** -0.5),
        "w2": nrm(ks[17], (DEPTH, N_EXPERTS, D_EXPERT, D_MODEL), D_EXPERT ** -0.5),
        "norm_final": 1.0 + nrm(ks[18], (D_MODEL,), 0.05),
    }


def reference(x, c, ctx, c_ctx, w_ada, b_ada, norm_attn, norm_ffn, w_in, w_out, lambda_qk, subln,
              sink, w_router, b_router, w1, w3, w2, norm_final):
    B, n, _ = x.shape
    cos, sin = axial_rope_tables(n)
    h_c = ctx
    for l in range(DEPTH):
        last = l == DEPTH - 1
        mod = (jax.nn.silu(c) @ w_ada[l] + b_ada[l])[:, None, :]
        mod_c = (jax.nn.silu(c_ctx) @ w_ada[l] + b_ada[l])[None, None, :]
        sh_a, sc_a, g_a, sh_f, sc_f, g_f = jnp.split(mod, 6, axis=-1)
        csh_a, csc_a, cg_a, csh_f, csc_f, cg_f = jnp.split(mod_c, 6, axis=-1)
        lam_init = 0.8 - 0.6 * math.exp(-0.3 * l)
        lq1, lk1, lq2, lk2 = (lambda_qk[l, i].astype(jnp.float32) for i in range(4))
        lam = jnp.exp(jnp.sum(lq1 * lk1)) - jnp.exp(jnp.sum(lq2 * lk2)) + lam_init

        hc = rmsnorm(h_c, norm_attn[l]) * (1 + csc_a) + csh_a
        cqa1, cqa2, cka1, cka2, cva, cqb, ckb, cvb = split_heads(hc @ w_in[l])

        hx = rmsnorm(x, norm_attn[l]) * (1 + sc_a) + sh_a
        mix = latent_mixers(hx, (cka1, cka2, cva, ckb, cvb), w_in[l], lam, lam_init, subln[l], sink[l], cos, sin)
        x = x + g_a * (mix @ w_out[l])
        hx = rmsnorm(x, norm_ffn[l]) * (1 + sc_f) + sh_f
        x = x + g_f * moe(hx, w_router, b_router, w1[l], w3[l], w2[l])

        if not last:
            oa_c = rmsnorm(diff_mix(cqa1, cqa2, cka1, cka2, cva, lam), subln[l]) * (1.0 - lam_init)
            ob_c = sink_attend(cqb, ckb, cvb, None, sink[l])
            mix_c = jnp.concatenate([oa_c.reshape(B, -1, HA * 2 * DA), ob_c], axis=-1)
            h_c = h_c + cg_a * (mix_c @ w_out[l])
            hc = rmsnorm(h_c, norm_ffn[l]) * (1 + csc_f) + csh_f
            h_c = h_c + cg_f * moe(hc, w_router, b_router, w1[l], w3[l], w2[l])
    return rmsnorm(x, norm_final)
```

```python
import functools
import math

import jax
import jax.numpy as jnp
from jax import lax
from jax.experimental import pallas as pl
from jax.experimental.pallas import tpu as pltpu

F32 = jnp.float32
BF16 = jnp.bfloat16

D_MODEL = 1024
HEAD_DIM = 64
HA = 4
HB = 8
KVB = 2
GB = HB // KVB
GRID_W = 64
WINDOW = 128
BLOCK = 128
ROPE_BASE = 10000.0
N_EXPERTS = 16
N_GROUPS = 4
EXPERTS_PER_GROUP = N_EXPERTS // N_GROUPS
D_EXPERT = 512
EPS = 1e-6
NEG = -1e30
LOG2E = 1.4426950408889634

LANES = 128
D_QA = HA * 2 * HEAD_DIM
D_IN = 2304
N_PROJ_TILES = 19
D_PROJ = N_PROJ_TILES * LANES
Q_TILES = (0, 1, 2, 3, 12, 13, 14, 15)
K_TILES = (4, 5, 6, 7, 16, 17)

VMEM_LIMIT = 56 * 1024 * 1024

PROJ_TM = 256
ATTN_TQ = 512
ATTN_KC = 512
MOE_TM = 1024
MOD_ROWS = 16


def _cparams(sem):
    return pltpu.CompilerParams(dimension_semantics=sem, vmem_limit_bytes=VMEM_LIMIT)


def _nt_dot(a, b):
    return lax.dot_general(a, b, (((1,), (1,)), ((), ())), preferred_element_type=F32)


def _split_bf16(x):
    hi = x.astype(BF16)
    lo = (x - hi.astype(F32)).astype(BF16)
    return hi, lo


def _rms(x):
    return x * lax.rsqrt(jnp.mean(x * x, axis=-1, keepdims=True) + EPS)


def _ada_kernel(c_ref, w_ref, b_ref, o_ref):
    c = c_ref[...]
    a = c * (1.0 / (1.0 + jnp.exp(-c)))
    a_hi, a_lo = _split_bf16(a)
    w_hi, w_lo = _split_bf16(w_ref[...])
    acc = jnp.dot(a_hi, w_hi, preferred_element_type=F32)
    acc += jnp.dot(a_hi, w_lo, preferred_element_type=F32)
    acc += jnp.dot(a_lo, w_hi, preferred_element_type=F32)
    o_ref[...] = acc + b_ref[...]


def _ada(c_all, w_ada, b_ada):
    depth = w_ada.shape[0]
    n_out = w_ada.shape[2]
    tn = D_MODEL
    return pl.pallas_call(
        _ada_kernel,
        grid=(depth, n_out // tn),
        in_specs=[
            pl.BlockSpec((MOD_ROWS, D_MODEL), lambda l, n: (0, 0)),
            pl.BlockSpec((None, D_MODEL, tn), lambda l, n: (l, 0, n)),
            pl.BlockSpec((None, 1, tn), lambda l, n: (l, 0, n)),
        ],
        out_specs=pl.BlockSpec((None, MOD_ROWS, tn), lambda l, n: (l, 0, n)),
        out_shape=jax.ShapeDtypeStruct((depth, MOD_ROWS, n_out), F32),
        name="ada_mod",
        compiler_params=_cparams(("parallel", "parallel")),
    )(c_all, w_ada, b_ada.reshape(depth, 1, n_out))


def _proj_kernel(x_ref, mod_ref, g_ref, w_ref, cq_ref, sq_ref, ck_ref, sk_ref, o_ref):
    x = x_ref[...]
    h = _rms(x) * g_ref[...] * (1.0 + mod_ref[1:2, :]) + mod_ref[0:1, :]
    acc = jnp.dot(h.astype(BF16), w_ref[...], preferred_element_type=F32)
    tm = x.shape[0]
    lane = lax.broadcasted_iota(jnp.int32, (tm, LANES), 1)
    first_half = (lane % 32) < 16
    for t in range(N_PROJ_TILES):
        a = acc[:, t * LANES:(t + 1) * LANES]
        if t in Q_TILES or t in K_TILES:
            cos, sin = (cq_ref, sq_ref) if t in Q_TILES else (ck_ref, sk_ref)
            partner = jnp.where(first_half, pltpu.roll(a, LANES - 16, 1), pltpu.roll(a, 16, 1))
            a = a * cos[...] + partner * sin[...]
        o_ref[:, t * LANES:(t + 1) * LANES] = a.astype(BF16)


def _proj(xs, mod, gnorm, w, tables, n_lat):
    b, s_tot, _ = xs.shape
    tm = PROJ_TM
    n_t = s_tot // tm
    ctx_t0 = n_lat // tm
    tab_spec = pl.BlockSpec((tm, LANES), lambda bi, t: (t, 0))
    return pl.pallas_call(
        _proj_kernel,
        grid=(b, n_t),
        in_specs=[
            pl.BlockSpec((None, tm, D_MODEL), lambda bi, t: (bi, t, 0)),
            pl.BlockSpec((None, 6, D_MODEL), lambda bi, t: (jnp.where(t >= ctx_t0, b, bi), 0, 0)),
            pl.BlockSpec((1, D_MODEL), lambda bi, t: (0, 0)),
            pl.BlockSpec((D_MODEL, D_PROJ), lambda bi, t: (0, 0)),
            tab_spec, tab_spec, tab_spec, tab_spec,
        ],
        out_specs=pl.BlockSpec((None, tm, D_PROJ), lambda bi, t: (bi, t, 0)),
        out_shape=jax.ShapeDtypeStruct((b, s_tot, D_PROJ), BF16),
        name="in_proj",
        compiler_params=_cparams(("parallel", "parallel")),
    )(xs, mod, gnorm, w, *tables)


def _transpose_into(vt_ref, v_ref, s_tot):
    for c in range(s_tot // LANES):
        blk = v_ref[c * LANES:(c + 1) * LANES, :].astype(F32)
        vt_ref[:, c * LANES:(c + 1) * LANES] = blk.T.astype(BF16)


def _attn_a_kernel(q_ref, k_ref, v_ref, lam_ref, subln_ref, o_ref, vt_ref, acc1_ref, acc2_ref,
                   *, n_lat, n_ctx, do_ctx, lam_init):
    s_tot = n_lat + n_ctx
    _transpose_into(vt_ref, v_ref, s_tot)

    lq = lam_ref[...]
    lam = (jnp.exp(jnp.sum(lq[0:1] * lq[1:2], axis=1, keepdims=True))
           - jnp.exp(jnp.sum(lq[2:3] * lq[3:4], axis=1, keepdims=True)) + lam_init)
    lane = lax.broadcasted_iota(jnp.int32, (1, LANES), 1)
    m_lo = (lane < HEAD_DIM).astype(BF16)
    m_hi = (lane >= HEAD_DIM).astype(BF16)
    subln = subln_ref[...]

    def finish(o1, l1, o2, l2):
        o = o1 * (1.0 / l1) - lam * (o2 * (1.0 / l2))
        o = o * lax.rsqrt(jnp.mean(o * o, axis=0, keepdims=True) + EPS)
        o = o * subln * (1.0 - lam_init)
        return o.T.astype(BF16)

    def one_shot(ks, vt, qm):
        s = _nt_dot(ks, qm)
        m = jnp.max(s, axis=0, keepdims=True)
        p = jnp.exp2(s - m)
        l = jnp.sum(p, axis=0, keepdims=True)
        return jnp.dot(vt, p.astype(BF16), preferred_element_type=F32), m, l

    tq, kc = ATTN_TQ, ATTN_KC
    n_kv_full = n_lat // kc

    def q_tile(i, carry):
        q0 = pl.multiple_of(i * tq, tq)
        qs = q_ref[pl.ds(q0, tq), :]
        q1 = qs * m_lo
        q2 = qs * m_hi
        kctx = k_ref[n_lat:s_tot, :]
        vctx = vt_ref[:, n_lat:s_tot]
        o1, m1, l1 = one_shot(kctx, vctx, q1)
        o2, m2, l2 = one_shot(kctx, vctx, q2)
        acc1_ref[...] = o1
        acc2_ref[...] = o2

        def kv_step(j, c):
            m1, l1, m2, l2 = c
            k0 = pl.multiple_of(j * kc, kc)
            ks = k_ref[pl.ds(k0, kc), :]
            vt = vt_ref[:, pl.ds(k0, kc)]
            out = []
            for qm, m, l, acc_ref in ((q1, m1, l1, acc1_ref), (q2, m2, l2, acc2_ref)):
                s = _nt_dot(ks, qm)
                m_new = jnp.maximum(m, jnp.max(s, axis=0, keepdims=True))
                alpha = jnp.exp2(m - m_new)
                p = jnp.exp2(s - m_new)
                l_new = alpha * l + jnp.sum(p, axis=0, keepdims=True)
                acc_ref[...] = alpha * acc_ref[...] + jnp.dot(vt, p.astype(BF16), preferred_element_type=F32)
                out += [m_new, l_new]
            return tuple(out)

        m1, l1, m2, l2 = lax.fori_loop(0, n_kv_full, kv_step, (m1, l1, m2, l2))
        o_ref[pl.ds(q0, tq), :] = finish(acc1_ref[...], l1, acc2_ref[...], l2)
        return carry

    lax.fori_loop(0, n_lat // tq, q_tile, 0)

    if do_ctx:
        qs = q_ref[n_lat:s_tot, :]
        kctx = k_ref[n_lat:s_tot, :]
        vctx = vt_ref[:, n_lat:s_tot]
        o1, _, l1 = one_shot(kctx, vctx, qs * m_lo)
        o2, _, l2 = one_shot(kctx, vctx, qs * m_hi)
        o_ref[n_lat:s_tot, :] = finish(o1, l1, o2, l2)
    else:
        o_ref[n_lat:s_tot, :] = jnp.zeros((n_ctx, LANES), BF16)


def _attn_a(proj, lam_qk, subln_col, n_lat, do_ctx, lam_init):
    b, s_tot, _ = proj.shape
    n_ctx = s_tot - n_lat
    kern = functools.partial(_attn_a_kernel, n_lat=n_lat, n_ctx=n_ctx, do_ctx=do_ctx, lam_init=lam_init)
    blk = lambda off: pl.BlockSpec((None, s_tot, LANES), lambda bi, h: (bi, 0, off + h))
    return pl.pallas_call(
        kern,
        grid=(b, HA),
        in_specs=[
            blk(0), blk(4), blk(8),
            pl.BlockSpec((4, HEAD_DIM), lambda bi, h: (0, 0)),
            pl.BlockSpec((2 * HEAD_DIM, 1), lambda bi, h: (0, 0)),
        ],
        out_specs=pl.BlockSpec((None, s_tot, LANES), lambda bi, h: (bi, 0, h)),
        out_shape=jax.ShapeDtypeStruct((b, s_tot, HA * LANES), BF16),
        scratch_shapes=[
            pltpu.VMEM((LANES, s_tot), BF16),
            pltpu.VMEM((LANES, ATTN_TQ), F32),
            pltpu.VMEM((LANES, ATTN_TQ), F32),
        ],
        name="attn_diff",
        compiler_params=_cparams(("parallel", "parallel")),
    )(proj, proj, proj, lam_qk, subln_col)


def _attn_b_kernel(sink_ref, q_ref, k_ref, v_ref, o_ref, vt_ref, *, n_lat, n_ctx, do_ctx):
    s_tot = n_lat + n_ctx
    j = pl.program_id(1)
    _transpose_into(vt_ref, v_ref, s_tot)

    lane = lax.broadcasted_iota(jnp.int32, (1, LANES), 1)
    m_lo = (lane < HEAD_DIM).astype(BF16)
    m_hi = (lane >= HEAD_DIM).astype(BF16)
    nq = BLOCK
    win = 3 * BLOCK
    sink_row = jnp.concatenate(
        [jnp.full((1, nq), sink_ref[j * GB + g] * LOG2E, F32) for g in range(GB)], axis=1)
    rc = (lax.broadcasted_iota(jnp.int32, (win, nq), 0) - lax.broadcasted_iota(jnp.int32, (win, nq), 1))
    v0 = pl.multiple_of(j * HEAD_DIM, HEAD_DIM)

    def q_stack(q0):
        qa = q_ref[pl.ds(q0, nq), 0:LANES]
        qb = q_ref[pl.ds(q0, nq), LANES:2 * LANES]
        return jnp.concatenate([qa * m_lo, qa * m_hi, qb * m_lo, qb * m_hi], axis=0)

    def emit(q0, o):
        for pr in range(2):
            pair = jnp.concatenate([o[:, (2 * pr) * nq:(2 * pr + 1) * nq],
                                    o[:, (2 * pr + 1) * nq:(2 * pr + 2) * nq]], axis=0)
            o_ref[pl.ds(q0, nq), pr * LANES:(pr + 1) * LANES] = pair.T.astype(BF16)

    kctx = k_ref[n_lat:s_tot, :]
    vctx = vt_ref[pl.ds(v0, HEAD_DIM), n_lat:s_tot]

    def lat_block(i, carry):
        q0 = pl.multiple_of(i * nq, nq)
        qs = q_stack(q0)
        start = pl.multiple_of(jnp.clip((i - 1) * nq, 0, n_lat - win), nq)
        kw = k_ref[pl.ds(start, win), :]
        vw = vt_ref[pl.ds(v0, HEAD_DIM), pl.ds(start, win)]
        s_c = _nt_dot(kctx, qs)
        s_w = _nt_dot(kw, qs)
        rel = rc + (start - i * nq)
        ok = (rel <= WINDOW) & (rel >= -WINDOW)
        ok4 = jnp.concatenate([ok] * GB, axis=1)
        s_w = jnp.where(ok4, s_w, NEG)
        m = jnp.maximum(jnp.maximum(jnp.max(s_c, axis=0, keepdims=True),
                                    jnp.max(s_w, axis=0, keepdims=True)), sink_row)
        p_c = jnp.exp2(s_c - m)
        p_w = jnp.exp2(s_w - m)
        l = (jnp.sum(p_c, axis=0, keepdims=True) + jnp.sum(p_w, axis=0, keepdims=True)
             + jnp.exp2(sink_row - m))
        o = (jnp.dot(vctx, p_c.astype(BF16), preferred_element_type=F32)
             + jnp.dot(vw, p_w.astype(BF16), preferred_element_type=F32))
        emit(q0, o * (1.0 / l))
        return carry

    lax.fori_loop(0, n_lat // nq, lat_block, 0)

    for cb in range(n_ctx // nq):
        q0 = n_lat + cb * nq
        if do_ctx:
            qs = q_stack(q0)
            s_c = _nt_dot(kctx, qs)
            m = jnp.maximum(jnp.max(s_c, axis=0, keepdims=True), sink_row)
            p_c = jnp.exp2(s_c - m)
            l = jnp.sum(p_c, axis=0, keepdims=True) + jnp.exp2(sink_row - m)
            o = jnp.dot(vctx, p_c.astype(BF16), preferred_element_type=F32)
            emit(q0, o * (1.0 / l))
        else:
            o_ref[q0:q0 + nq, :] = jnp.zeros((nq, 2 * LANES), BF16)


def _attn_b(proj, sink, n_lat, do_ctx):
    b, s_tot, _ = proj.shape
    n_ctx = s_tot - n_lat
    kern = functools.partial(_attn_b_kernel, n_lat=n_lat, n_ctx=n_ctx, do_ctx=do_ctx)
    return pl.pallas_call(
        kern,
        grid_spec=pltpu.PrefetchScalarGridSpec(
            num_scalar_prefetch=1,
            grid=(b, KVB),
            in_specs=[
                pl.BlockSpec((None, s_tot, 2 * LANES), lambda bi, j, s: (bi, 0, 6 + j)),
                pl.BlockSpec((None, s_tot, LANES), lambda bi, j, s: (bi, 0, 16 + j)),
                pl.BlockSpec((None, s_tot, LANES), lambda bi, j, s: (bi, 0, 18)),
            ],
            out_specs=pl.BlockSpec((None, s_tot, 2 * LANES), lambda bi, j, s: (bi, 0, j)),
            scratch_shapes=[pltpu.VMEM((LANES, s_tot), BF16)],
        ),
        out_shape=jax.ShapeDtypeStruct((b, s_tot, HB * HEAD_DIM), BF16),
        name="attn_win",
        compiler_params=_cparams(("parallel", "parallel")),
    )(sink, proj, proj, proj)


def _outproj_kernel(x_ref, oa_ref, ob_ref, w_ref, mod_ref, o_ref):
    acc = jnp.dot(oa_ref[...], w_ref[0:D_QA, :], preferred_element_type=F32)
    acc += jnp.dot(ob_ref[...], w_ref[D_QA:2 * D_QA, :], preferred_element_type=F32)
    o_ref[...] = x_ref[...] + mod_ref[2:3, :] * acc


def _outproj(xs, oa, ob, w, mod, n_lat, n_rows):
    b, s_tot, _ = xs.shape
    tm = PROJ_TM
    ctx_t0 = n_lat // tm
    return pl.pallas_call(
        _outproj_kernel,
        grid=(b, n_rows // tm),
        in_specs=[
            pl.BlockSpec((None, tm, D_MODEL), lambda bi, t: (bi, t, 0)),
            pl.BlockSpec((None, tm, D_QA), lambda bi, t: (bi, t, 0)),
            pl.BlockSpec((None, tm, D_QA), lambda bi, t: (bi, t, 0)),
            pl.BlockSpec((D_MODEL, D_MODEL), lambda bi, t: (0, 0)),
            pl.BlockSpec((None, 6, D_MODEL), lambda bi, t: (jnp.where(t >= ctx_t0, b, bi), 0, 0)),
        ],
        out_specs=pl.BlockSpec((None, tm, D_MODEL), lambda bi, t: (bi, t, 0)),
        out_shape=jax.ShapeDtypeStruct(xs.shape, F32),
        input_output_aliases={0: 0},
        name="out_proj",
        compiler_params=_cparams(("parallel", "parallel")),
    )(xs, oa, ob, w, mod)


def _route_t(logits_t, bias_col):
    s = 1.0 / (1.0 + jnp.exp(-logits_t))
    sb = s + bias_col
    s_rows = [s[e:e + 1, :] for e in range(N_EXPERTS)]
    sb_rows = [sb[e:e + 1, :] for e in range(N_EXPERTS)]
    best = None
    gsel = None
    for g in range(N_GROUPS):
        a, b, c, d = sb_rows[4 * g:4 * g + 4]
        hi1, lo1 = jnp.maximum(a, b), jnp.minimum(a, b)
        hi2, lo2 = jnp.maximum(c, d), jnp.minimum(c, d)
        gs = jnp.maximum(hi1, hi2) + jnp.maximum(jnp.minimum(hi1, hi2), jnp.maximum(lo1, lo2))
        if g == 0:
            best, gsel = gs, jnp.zeros_like(gs, dtype=jnp.int32)
        else:
            better = gs > best
            gsel = jnp.where(better, g, gsel)
            best = jnp.where(better, gs, best)
    masked = [jnp.where(gsel == (e // EXPERTS_PER_GROUP), sb_rows[e], -jnp.inf) for e in range(N_EXPERTS)]

    def argtop(vals):
        v, i = vals[0], jnp.zeros_like(gsel)
        for e in range(1, N_EXPERTS):
            better = vals[e] > v
            i = jnp.where(better, e, i)
            v = jnp.where(better, vals[e], v)
        return i

    i1 = argtop(masked)
    i2 = argtop([jnp.where(i1 == e, -jnp.inf, masked[e]) for e in range(N_EXPERTS)])
    w1 = sum(jnp.where(i1 == e, s_rows[e], 0.0) for e in range(N_EXPERTS))
    w2 = sum(jnp.where(i2 == e, s_rows[e], 0.0) for e in range(N_EXPERTS))
    inv = 1.0 / (w1 + w2)
    w1, w2 = w1 * inv, w2 * inv
    row = lax.broadcasted_iota(jnp.int32, logits_t.shape, 0)
    return jnp.where(row == i1, w1, 0.0) + jnp.where(row == i2, w2, 0.0)


def _moe_kernel(x_ref, mod_ref, g_ref, wr_ref, br_ref, w1_ref, w3_ref, w2_ref, gfin_ref, o_ref,
                hx_ref, comb_ref, acc_ref, *, tm, final):
    e = pl.program_id(2)

    @pl.when(e == 0)
    def _():
        x = x_ref[...].reshape(tm, D_MODEL)
        h = _rms(x) * g_ref[...] * (1.0 + mod_ref[4:5, :]) + mod_ref[3:4, :]
        hx_ref[...] = h.astype(BF16)
        h_hi, h_lo = _split_bf16(h)
        lt = _nt_dot(wr_ref[...], h_hi)
        lt2 = _nt_dot(wr_ref[0:N_EXPERTS, :], h_lo)
        logits_t = lt[0:N_EXPERTS] + lt[N_EXPERTS:2 * N_EXPERTS] + lt2
        comb_t = _route_t(logits_t, br_ref[...])
        comb_t = jnp.concatenate([comb_t, jnp.zeros((LANES - N_EXPERTS, tm), F32)], axis=0)
        comb_ref[...] = comb_t.T
        acc_ref[...] = jnp.zeros_like(acc_ref)

    lane = lax.broadcasted_iota(jnp.int32, (tm, LANES), 1)
    gate = jnp.sum(jnp.where(lane == e, comb_ref[...], 0.0), axis=1, keepdims=True)
    hx = hx_ref[...]
    a = jnp.dot(hx, w1_ref[...], preferred_element_type=F32)
    b = jnp.dot(hx, w3_ref[...], preferred_element_type=F32)
    u = (a * (1.0 / (1.0 + jnp.exp(-a)))) * b * gate
    acc_ref[...] += jnp.dot(u.astype(BF16), w2_ref[...], preferred_element_type=F32)

    @pl.when(e == N_EXPERTS - 1)
    def _():
        out = x_ref[...].reshape(tm, D_MODEL) + mod_ref[5:6, :] * acc_ref[...]
        if final:
            out = _rms(out) * gfin_ref[...]
        o_ref[...] = out.reshape(o_ref.shape)


def _moe(xs, mod, gnorm, wr_t, br_col, w1, w3, w2, gfin, *, ctx_part, n_lat, final):
    b, s_tot, _ = xs.shape
    n_ctx = s_tot - n_lat
    if ctx_part:
        nb = min(b, MOE_TM // n_ctx)
        tm = nb * n_ctx
        grid = (b // nb, 1, N_EXPERTS)
        x_spec = pl.BlockSpec((nb, n_ctx, D_MODEL), lambda bi, t, e: (bi, n_lat // n_ctx, 0))
        mod_spec = pl.BlockSpec((None, 6, D_MODEL), lambda bi, t, e: (b, 0, 0))
        out_shape = jax.ShapeDtypeStruct(xs.shape, F32)
        aliases = {0: 0}
    else:
        tm = MOE_TM
        grid = (b, n_lat // tm, N_EXPERTS)
        x_spec = pl.BlockSpec((1, tm, D_MODEL), lambda bi, t, e: (bi, t, 0))
        mod_spec = pl.BlockSpec((None, 6, D_MODEL), lambda bi, t, e: (bi, 0, 0))
        if final:
            out_shape = jax.ShapeDtypeStruct((b, n_lat, D_MODEL), F32)
            aliases = {}
        else:
            out_shape = jax.ShapeDtypeStruct(xs.shape, F32)
            aliases = {0: 0}
    kern = functools.partial(_moe_kernel, tm=tm, final=final)
    const = lambda shape: pl.BlockSpec(shape, lambda bi, t, e: (0,) * len(shape))
    return pl.pallas_call(
        kern,
        grid=grid,
        in_specs=[
            x_spec, mod_spec,
            const((1, D_MODEL)),
            const((2 * N_EXPERTS, D_MODEL)),
            const((N_EXPERTS, 1)),
            pl.BlockSpec((None, D_MODEL, D_EXPERT), lambda bi, t, e: (e, 0, 0)),
            pl.BlockSpec((None, D_MODEL, D_EXPERT), lambda bi, t, e: (e, 0, 0)),
            pl.BlockSpec((None, D_EXPERT, D_MODEL), lambda bi, t, e: (e, 0, 0)),
            const((1, D_MODEL)),
        ],
        out_specs=x_spec,
        out_shape=out_shape,
        scratch_shapes=[
            pltpu.VMEM((tm, D_MODEL), BF16),
            pltpu.VMEM((tm, LANES), F32),
            pltpu.VMEM((tm, D_MODEL), F32),
        ],
        input_output_aliases=aliases,
        name="moe_ctx" if ctx_part else "moe_lat",
        compiler_params=_cparams(("parallel", "parallel", "arbitrary")),
    )(xs, mod, gnorm, wr_t, br_col, w1, w3, w2, gfin)


def _rope_tables(n_lat, n_ctx):
    pos = jnp.arange(n_lat)
    row = (pos // GRID_W).astype(F32)
    col = (pos % GRID_W).astype(F32)
    n_freq = HEAD_DIM // 4
    inv = 1.0 / (ROPE_BASE ** (jnp.arange(n_freq, dtype=F32) / n_freq))
    lane = jnp.arange(LANES) % HEAD_DIM
    axis = lane // 32
    second = (lane % 32) // 16
    freq = inv[lane % 16]
    ang = jnp.where(axis[None, :] == 0, row[:, None], col[:, None]) * freq[None, :]
    cos = jnp.cos(ang)
    sin = jnp.sin(ang) * jnp.where(second == 0, -1.0, 1.0)[None, :]
    cos = jnp.concatenate([cos, jnp.ones((n_ctx, LANES), F32)], axis=0)
    sin = jnp.concatenate([sin, jnp.zeros((n_ctx, LANES), F32)], axis=0)
    q_scale = HEAD_DIM ** -0.5 * LOG2E
    return cos * q_scale, sin * q_scale, cos, sin


def _proj_cols():
    base = jnp.arange(D_IN)
    kb0 = base[2048:2112]
    kb1 = base[2112:2176]
    return jnp.concatenate([base[:2048], kb0, kb0, kb1, kb1, base[2176:]])


def kernel(x, c, ctx, c_ctx, w_ada, b_ada, norm_attn, norm_ffn, w_in, w_out, lambda_qk, subln, sink, w_router,
           b_router, w1, w3, w2, norm_final):
    b, n_lat, _ = x.shape
    n_ctx = ctx.shape[1]
    depth = w_in.shape[0]

    xs = jnp.concatenate([x, ctx], axis=1)
    c_all = jnp.zeros((MOD_ROWS, D_MODEL), F32).at[:b].set(c).at[b].set(c_ctx)
    mod_all = _ada(c_all, w_ada, b_ada).reshape(depth, MOD_ROWS, 6, D_MODEL)
    tables = _rope_tables(n_lat, n_ctx)
    cols = _proj_cols()

    wr_hi, wr_lo = _split_bf16(w_router.T)
    wr_t = jnp.concatenate([wr_hi, wr_lo], axis=0)
    br_col = b_router.reshape(N_EXPERTS, 1)
    gfin = norm_final.reshape(1, D_MODEL)

    out = None
    for l in range(depth):
        last = l == depth - 1
        lam_init = 0.8 - 0.6 * math.exp(-0.3 * l)
        mod = mod_all[l]
        w_in_l = w_in[l][:, cols].astype(BF16)
        proj = _proj(xs, mod, norm_attn[l].reshape(1, D_MODEL), w_in_l, tables, n_lat)
        oa = _attn_a(proj, lambda_qk[l], subln[l].reshape(2 * HEAD_DIM, 1), n_lat, not last, lam_init)
        ob = _attn_b(proj, sink[l], n_lat, not last)
        n_rows = n_lat if last else n_lat + n_ctx
        xs = _outproj(xs, oa, ob, w_out[l].astype(BF16), mod, n_lat, n_rows)
        gn = norm_ffn[l].reshape(1, D_MODEL)
        w1b, w3b, w2b = w1[l].astype(BF16), w3[l].astype(BF16), w2[l].astype(BF16)
        if last:
            out = _moe(xs, mod, gn, wr_t, br_col, w1b, w3b, w2b, gfin, ctx_part=False, n_lat=n_lat, final=True)
        else:
            xs = _moe(xs, mod, gn, wr_t, br_col, w1b, w3b, w2b, gfin, ctx_part=False, n_lat=n_lat, final=False)
            xs = _moe(xs, mod, gn, wr_t, br_col, w1b, w3b, w2b, gfin, ctx_part=True, n_lat=n_lat, final=False)
    return out
```

```python
import functools
import math

import jax
import jax.numpy as jnp
from jax import lax
from jax.experimental import pallas as pl
from jax.experimental.pallas import tpu as pltpu

F32 = jnp.float32
BF16 = jnp.bfloat16

D_MODEL = 1024
HEAD_DIM = 64
HA = 4
HB = 8
KVB = 2
GB = HB // KVB
GRID_W = 64
WINDOW = 128
BLOCK = 128
ROPE_BASE = 10000.0
N_EXPERTS = 16
N_GROUPS = 4
EXPERTS_PER_GROUP = N_EXPERTS // N_GROUPS
D_EXPERT = 512
EPS = 1e-6
NEG = -1e30
LOG2E = 1.4426950408889634

LANES = 128
D_QA = HA * 2 * HEAD_DIM
D_IN = 2304
N_PROJ_TILES = 19
D_PROJ = N_PROJ_TILES * LANES
Q_TILES = (0, 1, 2, 3, 12, 13, 14, 15)
K_TILES = (4, 5, 6, 7, 16, 17)

VMEM_LIMIT = 56 * 1024 * 1024

PROJ_TM = 256
ATTN_TQ = 512
ATTN_KC = 512
ROUTE_TM = 256
MOE_TMG = 512
ROW_W = D_MODEL + LANES
MOD_ROWS = 16


def _cparams(sem):
    return pltpu.CompilerParams(dimension_semantics=sem, vmem_limit_bytes=VMEM_LIMIT)


def _nt_dot(a, b):
    return lax.dot_general(a, b, (((1,), (1,)), ((), ())), preferred_element_type=F32)


def _split_bf16(x):
    hi = x.astype(BF16)
    lo = (x - hi.astype(F32)).astype(BF16)
    return hi, lo


def _rms(x):
    return x * lax.rsqrt(jnp.mean(x * x, axis=-1, keepdims=True) + EPS)


def _ada_kernel(c_ref, w_ref, b_ref, o_ref):
    c = c_ref[...]
    a = c * (1.0 / (1.0 + jnp.exp(-c)))
    a_hi, a_lo = _split_bf16(a)
    w_hi, w_lo = _split_bf16(w_ref[...])
    acc = jnp.dot(a_hi, w_hi, preferred_element_type=F32)
    acc += jnp.dot(a_hi, w_lo, preferred_element_type=F32)
    acc += jnp.dot(a_lo, w_hi, preferred_element_type=F32)
    o_ref[...] = acc + b_ref[...]


def _ada(c_all, w_ada, b_ada):
    depth = w_ada.shape[0]
    n_out = w_ada.shape[2]
    tn = D_MODEL
    return pl.pallas_call(
        _ada_kernel,
        grid=(depth, n_out // tn),
        in_specs=[
            pl.BlockSpec((MOD_ROWS, D_MODEL), lambda l, n: (0, 0)),
            pl.BlockSpec((None, D_MODEL, tn), lambda l, n: (l, 0, n)),
            pl.BlockSpec((None, 1, tn), lambda l, n: (l, 0, n)),
        ],
        out_specs=pl.BlockSpec((None, MOD_ROWS, tn), lambda l, n: (l, 0, n)),
        out_shape=jax.ShapeDtypeStruct((depth, MOD_ROWS, n_out), F32),
        name="ada_mod",
        compiler_params=_cparams(("parallel", "parallel")),
    )(c_all, w_ada, b_ada.reshape(depth, 1, n_out))


def _proj_kernel(x_ref, mod_ref, g_ref, w_ref, cq_ref, sq_ref, ck_ref, sk_ref, o_ref):
    x = x_ref[...]
    h = _rms(x) * g_ref[...] * (1.0 + mod_ref[1:2, :]) + mod_ref[0:1, :]
    acc = jnp.dot(h.astype(BF16), w_ref[...], preferred_element_type=F32)
    tm = x.shape[0]
    lane = lax.broadcasted_iota(jnp.int32, (tm, LANES), 1)
    first_half = (lane % 32) < 16
    for t in range(N_PROJ_TILES):
        a = acc[:, t * LANES:(t + 1) * LANES]
        if t in Q_TILES or t in K_TILES:
            cos, sin = (cq_ref, sq_ref) if t in Q_TILES else (ck_ref, sk_ref)
            partner = jnp.where(first_half, pltpu.roll(a, LANES - 16, 1), pltpu.roll(a, 16, 1))
            a = a * cos[...] + partner * sin[...]
        o_ref[:, t * LANES:(t + 1) * LANES] = a.astype(BF16)


def _proj(xs, mod, gnorm, w, tables, n_lat):
    b, s_tot, _ = xs.shape
    tm = PROJ_TM
    n_t = s_tot // tm
    ctx_t0 = n_lat // tm
    tab_spec = pl.BlockSpec((tm, LANES), lambda bi, t: (t, 0))
    return pl.pallas_call(
        _proj_kernel,
        grid=(b, n_t),
        in_specs=[
            pl.BlockSpec((None, tm, D_MODEL), lambda bi, t: (bi, t, 0)),
            pl.BlockSpec((None, 6, D_MODEL), lambda bi, t: (jnp.where(t >= ctx_t0, b, bi), 0, 0)),
            pl.BlockSpec((1, D_MODEL), lambda bi, t: (0, 0)),
            pl.BlockSpec((D_MODEL, D_PROJ), lambda bi, t: (0, 0)),
            tab_spec, tab_spec, tab_spec, tab_spec,
        ],
        out_specs=pl.BlockSpec((None, tm, D_PROJ), lambda bi, t: (bi, t, 0)),
        out_shape=jax.ShapeDtypeStruct((b, s_tot, D_PROJ), BF16),
        name="in_proj",
        compiler_params=_cparams(("parallel", "parallel")),
    )(xs, mod, gnorm, w, *tables)


def _transpose_into(vt_ref, v_ref, s_tot):
    for c in range(s_tot // LANES):
        blk = v_ref[c * LANES:(c + 1) * LANES, :].astype(F32)
        vt_ref[:, c * LANES:(c + 1) * LANES] = blk.T.astype(BF16)


def _attn_a_kernel(q_ref, k_ref, v_ref, lam_ref, subln_ref, o_ref, vt_ref, acc1_ref, acc2_ref,
                   s0_ref, s1_ref, sc_ref, *, n_lat, n_ctx, do_ctx, lam_init):
    s_tot = n_lat + n_ctx
    _transpose_into(vt_ref, v_ref, s_tot)

    lq = lam_ref[...]
    lam = (jnp.exp(jnp.sum(lq[0:1] * lq[1:2], axis=1, keepdims=True))
           - jnp.exp(jnp.sum(lq[2:3] * lq[3:4], axis=1, keepdims=True)) + lam_init)
    lane = lax.broadcasted_iota(jnp.int32, (1, LANES), 1)
    m_lo = (lane < HEAD_DIM).astype(BF16)
    m_hi = (lane >= HEAD_DIM).astype(BF16)
    subln = subln_ref[...]

    def finish(o1, l1, o2, l2):
        o = o1 * (1.0 / l1) - lam * (o2 * (1.0 / l2))
        o = o * lax.rsqrt(jnp.mean(o * o, axis=0, keepdims=True) + EPS)
        o = o * subln * (1.0 - lam_init)
        return o.T.astype(BF16)

    def one_shot(ks, vt, qm):
        s = _nt_dot(ks, qm)
        m = jnp.max(s, axis=0, keepdims=True)
        p = jnp.exp2(s - m)
        l = jnp.sum(p, axis=0, keepdims=True)
        return jnp.dot(vt, p.astype(BF16), preferred_element_type=F32), m, l

    tq, kc = ATTN_TQ, ATTN_KC
    n_kv = n_lat // kc

    def q_tile(i, carry):
        q0 = pl.multiple_of(i * tq, tq)
        qs = q_ref[pl.ds(q0, tq), :]
        qm = (qs * m_lo, qs * m_hi)

        def score(k0, size, dst_ref):
            ks = k_ref[pl.ds(k0, size), :]
            dst_ref[0] = _nt_dot(ks, qm[0])
            dst_ref[1] = _nt_dot(ks, qm[1])

        def update(src_ref, k0, size, c):
            vt = vt_ref[:, pl.ds(k0, size)]
            out = []
            for mp, acc_ref in ((0, acc1_ref), (1, acc2_ref)):
                s = src_ref[mp]
                smax = jnp.max(s, axis=0, keepdims=True)
                if c is None:
                    m_new = smax
                    p = jnp.exp2(s - m_new)
                    l_new = jnp.sum(p, axis=0, keepdims=True)
                    acc_ref[...] = jnp.dot(vt, p.astype(BF16), preferred_element_type=F32)
                else:
                    m, l = c[2 * mp], c[2 * mp + 1]
                    m_new = jnp.maximum(m, smax)
                    alpha = jnp.exp2(m - m_new)
                    p = jnp.exp2(s - m_new)
                    l_new = alpha * l + jnp.sum(p, axis=0, keepdims=True)
                    acc_ref[...] = alpha * acc_ref[...] + jnp.dot(vt, p.astype(BF16),
                                                                  preferred_element_type=F32)
                out += [m_new, l_new]
            return tuple(out)

        score(n_lat, n_ctx, sc_ref)
        score(0, kc, s0_ref)
        c = update(sc_ref, n_lat, n_ctx, None)

        def pair(t, c):
            ka = pl.multiple_of(2 * t * kc, kc)
            kb = pl.multiple_of(ka + kc, kc)
            score(kb, kc, s1_ref)
            c = update(s0_ref, ka, kc, c)
            score(pl.multiple_of(kb + kc, kc), kc, s0_ref)
            return update(s1_ref, kb, kc, c)

        c = lax.fori_loop(0, n_kv // 2 - 1, pair, c)
        score((n_kv - 1) * kc, kc, s1_ref)
        c = update(s0_ref, (n_kv - 2) * kc, kc, c)
        m1, l1, m2, l2 = update(s1_ref, (n_kv - 1) * kc, kc, c)
        o_ref[pl.ds(q0, tq), :] = finish(acc1_ref[...], l1, acc2_ref[...], l2)
        return carry

    lax.fori_loop(0, n_lat // tq, q_tile, 0)

    if do_ctx:
        qs = q_ref[n_lat:s_tot, :]
        kctx = k_ref[n_lat:s_tot, :]
        vctx = vt_ref[:, n_lat:s_tot]
        o1, _, l1 = one_shot(kctx, vctx, qs * m_lo)
        o2, _, l2 = one_shot(kctx, vctx, qs * m_hi)
        o_ref[n_lat:s_tot, :] = finish(o1, l1, o2, l2)
    else:
        o_ref[n_lat:s_tot, :] = jnp.zeros((n_ctx, LANES), BF16)


def _attn_a(proj, lam_qk, subln_col, n_lat, do_ctx, lam_init):
    b, s_tot, _ = proj.shape
    n_ctx = s_tot - n_lat
    kern = functools.partial(_attn_a_kernel, n_lat=n_lat, n_ctx=n_ctx, do_ctx=do_ctx, lam_init=lam_init)
    blk = lambda off: pl.BlockSpec((None, s_tot, LANES), lambda bi, h: (bi, 0, off + h))
    return pl.pallas_call(
        kern,
        grid=(b, HA),
        in_specs=[
            blk(0), blk(4), blk(8),
            pl.BlockSpec((4, HEAD_DIM), lambda bi, h: (0, 0)),
            pl.BlockSpec((2 * HEAD_DIM, 1), lambda bi, h: (0, 0)),
        ],
        out_specs=pl.BlockSpec((None, s_tot, LANES), lambda bi, h: (bi, 0, h)),
        out_shape=jax.ShapeDtypeStruct((b, s_tot, HA * LANES), BF16),
        scratch_shapes=[
            pltpu.VMEM((LANES, s_tot), BF16),
            pltpu.VMEM((LANES, ATTN_TQ), F32),
            pltpu.VMEM((LANES, ATTN_TQ), F32),
            pltpu.VMEM((2, ATTN_KC, ATTN_TQ), F32),
            pltpu.VMEM((2, ATTN_KC, ATTN_TQ), F32),
            pltpu.VMEM((2, n_ctx, ATTN_TQ), F32),
        ],
        name="attn_diff",
        compiler_params=_cparams(("parallel", "parallel")),
    )(proj, proj, proj, lam_qk, subln_col)


def _attn_b_kernel(sink_ref, q_ref, k_ref, v_ref, o_ref, vt_ref, sc0_ref, sw0_ref, sc1_ref, sw1_ref,
                   *, n_lat, n_ctx, do_ctx):
    s_tot = n_lat + n_ctx
    j = pl.program_id(1)
    _transpose_into(vt_ref, v_ref, s_tot)

    lane = lax.broadcasted_iota(jnp.int32, (1, LANES), 1)
    m_lo = (lane < HEAD_DIM).astype(BF16)
    m_hi = (lane >= HEAD_DIM).astype(BF16)
    nq = BLOCK
    win = 3 * BLOCK
    sink_row = jnp.concatenate(
        [jnp.full((1, nq), sink_ref[j * GB + g] * LOG2E, F32) for g in range(GB)], axis=1)
    rc = (lax.broadcasted_iota(jnp.int32, (win, nq), 0) - lax.broadcasted_iota(jnp.int32, (win, nq), 1))
    v0 = pl.multiple_of(j * HEAD_DIM, HEAD_DIM)

    def q_stack(q0):
        qa = q_ref[pl.ds(q0, nq), 0:LANES]
        qb = q_ref[pl.ds(q0, nq), LANES:2 * LANES]
        return jnp.concatenate([qa * m_lo, qa * m_hi, qb * m_lo, qb * m_hi], axis=0)

    def emit(q0, o):
        for pr in range(2):
            pair = jnp.concatenate([o[:, (2 * pr) * nq:(2 * pr + 1) * nq],
                                    o[:, (2 * pr + 1) * nq:(2 * pr + 2) * nq]], axis=0)
            o_ref[pl.ds(q0, nq), pr * LANES:(pr + 1) * LANES] = pair.T.astype(BF16)

    kctx = k_ref[n_lat:s_tot, :]
    vctx = vt_ref[pl.ds(v0, HEAD_DIM), n_lat:s_tot]

    def win_start(i):
        return pl.multiple_of(jnp.clip((i - 1) * nq, 0, n_lat - win), nq)

    def score(i, sc_ref, sw_ref):
        qs = q_stack(pl.multiple_of(i * nq, nq))
        start = win_start(i)
        sc_ref[...] = _nt_dot(kctx, qs)
        s_w = _nt_dot(k_ref[pl.ds(start, win), :], qs)
        rel = rc + (start - i * nq)
        ok = (rel <= WINDOW) & (rel >= -WINDOW)
        sw_ref[...] = jnp.where(jnp.concatenate([ok] * GB, axis=1), s_w, NEG)

    def attend(i, sc_ref, sw_ref):
        vw = vt_ref[pl.ds(v0, HEAD_DIM), pl.ds(win_start(i), win)]
        s_c = sc_ref[...]
        s_w = sw_ref[...]
        m = jnp.maximum(jnp.maximum(jnp.max(s_c, axis=0, keepdims=True),
                                    jnp.max(s_w, axis=0, keepdims=True)), sink_row)
        p_c = jnp.exp2(s_c - m)
        p_w = jnp.exp2(s_w - m)
        l = (jnp.sum(p_c, axis=0, keepdims=True) + jnp.sum(p_w, axis=0, keepdims=True)
             + jnp.exp2(sink_row - m))
        o = (jnp.dot(vctx, p_c.astype(BF16), preferred_element_type=F32)
             + jnp.dot(vw, p_w.astype(BF16), preferred_element_type=F32))
        emit(pl.multiple_of(i * nq, nq), o * (1.0 / l))

    n_blk = n_lat // nq
    score(0, sc0_ref, sw0_ref)

    def pair(t, carry):
        score(2 * t + 1, sc1_ref, sw1_ref)
        attend(2 * t, sc0_ref, sw0_ref)
        score(2 * t + 2, sc0_ref, sw0_ref)
        attend(2 * t + 1, sc1_ref, sw1_ref)
        return carry

    lax.fori_loop(0, n_blk // 2 - 1, pair, 0)
    score(n_blk - 1, sc1_ref, sw1_ref)
    attend(n_blk - 2, sc0_ref, sw0_ref)
    attend(n_blk - 1, sc1_ref, sw1_ref)

    for cb in range(n_ctx // nq):
        q0 = n_lat + cb * nq
        if do_ctx:
            qs = q_stack(q0)
            s_c = _nt_dot(kctx, qs)
            m = jnp.maximum(jnp.max(s_c, axis=0, keepdims=True), sink_row)
            p_c = jnp.exp2(s_c - m)
            l = jnp.sum(p_c, axis=0, keepdims=True) + jnp.exp2(sink_row - m)
            o = jnp.dot(vctx, p_c.astype(BF16), preferred_element_type=F32)
            emit(q0, o * (1.0 / l))
        else:
            o_ref[q0:q0 + nq, :] = jnp.zeros((nq, 2 * LANES), BF16)


def _attn_b(proj, sink, n_lat, do_ctx):
    b, s_tot, _ = proj.shape
    n_ctx = s_tot - n_lat
    kern = functools.partial(_attn_b_kernel, n_lat=n_lat, n_ctx=n_ctx, do_ctx=do_ctx)
    return pl.pallas_call(
        kern,
        grid_spec=pltpu.PrefetchScalarGridSpec(
            num_scalar_prefetch=1,
            grid=(b, KVB),
            in_specs=[
                pl.BlockSpec((None, s_tot, 2 * LANES), lambda bi, j, s: (bi, 0, 6 + j)),
                pl.BlockSpec((None, s_tot, LANES), lambda bi, j, s: (bi, 0, 16 + j)),
                pl.BlockSpec((None, s_tot, LANES), lambda bi, j, s: (bi, 0, 18)),
            ],
            out_specs=pl.BlockSpec((None, s_tot, 2 * LANES), lambda bi, j, s: (bi, 0, j)),
            scratch_shapes=[
                pltpu.VMEM((LANES, s_tot), BF16),
                pltpu.VMEM((n_ctx, GB * BLOCK), F32),
                pltpu.VMEM((3 * BLOCK, GB * BLOCK), F32),
                pltpu.VMEM((n_ctx, GB * BLOCK), F32),
                pltpu.VMEM((3 * BLOCK, GB * BLOCK), F32),
            ],
        ),
        out_shape=jax.ShapeDtypeStruct((b, s_tot, HB * HEAD_DIM), BF16),
        name="attn_win",
        compiler_params=_cparams(("parallel", "parallel")),
    )(sink, proj, proj, proj)


def _outproj_kernel(x_ref, oa_ref, ob_ref, w_ref, mod_ref, o_ref):
    acc = jnp.dot(oa_ref[...], w_ref[0:D_QA, :], preferred_element_type=F32)
    acc += jnp.dot(ob_ref[...], w_ref[D_QA:2 * D_QA, :], preferred_element_type=F32)
    o_ref[...] = x_ref[...] + mod_ref[2:3, :] * acc


def _outproj(xs, oa, ob, w, mod, n_lat, n_rows):
    b, s_tot, _ = xs.shape
    tm = PROJ_TM
    ctx_t0 = n_lat // tm
    return pl.pallas_call(
        _outproj_kernel,
        grid=(b, n_rows // tm),
        in_specs=[
            pl.BlockSpec((None, tm, D_MODEL), lambda bi, t: (bi, t, 0)),
            pl.BlockSpec((None, tm, D_QA), lambda bi, t: (bi, t, 0)),
            pl.BlockSpec((None, tm, D_QA), lambda bi, t: (bi, t, 0)),
            pl.BlockSpec((D_MODEL, D_MODEL), lambda bi, t: (0, 0)),
            pl.BlockSpec((None, 6, D_MODEL), lambda bi, t: (jnp.where(t >= ctx_t0, b, bi), 0, 0)),
        ],
        out_specs=pl.BlockSpec((None, tm, D_MODEL), lambda bi, t: (bi, t, 0)),
        out_shape=jax.ShapeDtypeStruct(xs.shape, F32),
        input_output_aliases={0: 0},
        name="out_proj",
        compiler_params=_cparams(("parallel", "parallel")),
    )(xs, oa, ob, w, mod)


def _route_t(logits_t, bias_col):
    s = 1.0 / (1.0 + jnp.exp(-logits_t))
    sb = s + bias_col
    s_rows = [s[e:e + 1, :] for e in range(N_EXPERTS)]
    sb_rows = [sb[e:e + 1, :] for e in range(N_EXPERTS)]
    best = None
    gsel = None
    for g in range(N_GROUPS):
        a, b, c, d = sb_rows[4 * g:4 * g + 4]
        hi1, lo1 = jnp.maximum(a, b), jnp.minimum(a, b)
        hi2, lo2 = jnp.maximum(c, d), jnp.minimum(c, d)
        gs = jnp.maximum(hi1, hi2) + jnp.maximum(jnp.minimum(hi1, hi2), jnp.maximum(lo1, lo2))
        if g == 0:
            best, gsel = gs, jnp.zeros_like(gs, dtype=jnp.int32)
        else:
            better = gs > best
            gsel = jnp.where(better, g, gsel)
            best = jnp.where(better, gs, best)
    masked = [jnp.where(gsel == (e // EXPERTS_PER_GROUP), sb_rows[e], -jnp.inf) for e in range(N_EXPERTS)]

    def argtop(vals):
        v, i = vals[0], jnp.zeros_like(gsel)
        for e in range(1, N_EXPERTS):
            better = vals[e] > v
            i = jnp.where(better, e, i)
            v = jnp.where(better, vals[e], v)
        return i

    i1 = argtop(masked)
    i2 = argtop([jnp.where(i1 == e, -jnp.inf, masked[e]) for e in range(N_EXPERTS)])
    w1 = sum(jnp.where(i1 == e, s_rows[e], 0.0) for e in range(N_EXPERTS))
    w2 = sum(jnp.where(i2 == e, s_rows[e], 0.0) for e in range(N_EXPERTS))
    inv = 1.0 / (w1 + w2)
    w1, w2 = w1 * inv, w2 * inv
    first = gsel * EXPERTS_PER_GROUP
    gates = [jnp.where(i1 == first + k, w1, 0.0) + jnp.where(i2 == first + k, w2, 0.0)
             for k in range(EXPERTS_PER_GROUP)]
    return gsel, gates


def _route_kernel(x_ref, mod_ref, g_ref, wr_ref, br_ref, tri_ref, row_ref, info_ref, cnt_ref, base_ref):
    tm = x_ref.shape[0]

    @pl.when((pl.program_id(0) == 0) & (pl.program_id(1) == 0))
    def _():
        base_ref[...] = jnp.zeros_like(base_ref)

    x = x_ref[...]
    h = _rms(x) * g_ref[...] * (1.0 + mod_ref[4:5, :]) + mod_ref[3:4, :]
    h_hi, h_lo = _split_bf16(h)
    lt = _nt_dot(wr_ref[...], h_hi)
    lt2 = _nt_dot(wr_ref[0:N_EXPERTS, :], h_lo)
    logits_t = lt[0:N_EXPERTS] + lt[N_EXPERTS:2 * N_EXPERTS] + lt2
    gsel, gates = _route_t(logits_t, br_ref[...])

    row128 = lax.broadcasted_iota(jnp.int32, (LANES, tm), 0)
    gate_t = jnp.zeros((LANES, tm), F32)
    for k in range(EXPERTS_PER_GROUP):
        gate_t = jnp.where(row128 == k, gates[k], gate_t)
    row_ref[:, 0:D_MODEL] = h
    row_ref[:, D_MODEL:ROW_W] = gate_t.T

    row8 = lax.broadcasted_iota(jnp.int32, (8, tm), 0)
    member = row8 == gsel
    ranks = jnp.dot(member.astype(BF16), tri_ref[...], preferred_element_type=F32) + base_ref[:, 0:1]
    rank = jnp.sum(jnp.where(member, ranks, 0.0), axis=0, keepdims=True)
    info_ref[...] = jnp.where(row8 == 0, gsel.astype(F32), jnp.where(row8 == 1, rank, 0.0))
    base_ref[...] = base_ref[...] + jnp.sum(member.astype(F32), axis=1, keepdims=True)
    cnt_ref[...] = base_ref[...]


def _route(xs, mod, gnorm, wr_t, br_col, n_lat, n_rows):
    b, s_tot, _ = xs.shape
    tm = ROUTE_TM
    n_t = n_rows // tm
    ctx_t0 = n_lat // tm
    tri = (jnp.arange(tm)[:, None] < jnp.arange(tm)[None, :]).astype(BF16)
    const = lambda shape: pl.BlockSpec(shape, lambda bi, t: (0,) * len(shape))
    return pl.pallas_call(
        _route_kernel,
        grid=(b, n_t),
        in_specs=[
            pl.BlockSpec((None, tm, D_MODEL), lambda bi, t: (bi, t, 0)),
            pl.BlockSpec((None, 6, D_MODEL), lambda bi, t: (jnp.where(t >= ctx_t0, b, bi), 0, 0)),
            const((1, D_MODEL)),
            const((2 * N_EXPERTS, D_MODEL)),
            const((N_EXPERTS, 1)),
            const((tm, tm)),
        ],
        out_specs=[
            pl.BlockSpec((None, tm, ROW_W), lambda bi, t: (bi, t, 0)),
            pl.BlockSpec((None, 8, tm), lambda bi, t: (bi, 0, t)),
            const((8, LANES)),
        ],
        out_shape=[
            jax.ShapeDtypeStruct((b, n_t * tm, ROW_W), F32),
            jax.ShapeDtypeStruct((b, 8, n_t * tm), F32),
            jax.ShapeDtypeStruct((8, LANES), F32),
        ],
        scratch_shapes=[pltpu.VMEM((8, LANES), F32)],
        name="moe_route",
        compiler_params=_cparams(("arbitrary", "arbitrary")),
    )(xs, mod, gnorm, wr_t, br_col, tri)


def _row_copy(src, dst, sem):
    return pltpu.make_async_copy(src, dst, sem)


def _permute_kernel(ends_ref, pos_ref, row_ref, xs_hbm, zero_ref, sem, zsem):
    tm = row_ref.shape[0]

    @pl.when((pl.program_id(0) == 0) & (pl.program_id(1) == 0))
    def _():
        zero_ref[...] = jnp.zeros_like(zero_ref)
        for g in range(N_GROUPS):
            lo = ends_ref[g] if g == 0 else ends_ref[g] - ends_ref[g - 1]

            @pl.when(lo > 0)
            def _():
                dst = xs_hbm.at[pl.ds(pl.multiple_of(ends_ref[g] - MOE_TMG, MOE_TMG), MOE_TMG), :]
                cp = _row_copy(zero_ref, dst, zsem)
                cp.start()
                cp.wait()

        n_tiles = xs_hbm.shape[0] // MOE_TMG
        for t0 in range(n_tiles - N_GROUPS, n_tiles):
            @pl.when(t0 * MOE_TMG >= ends_ref[N_GROUPS - 1])
            def _():
                cp = _row_copy(zero_ref, xs_hbm.at[pl.ds(t0 * MOE_TMG, MOE_TMG), :], zsem)
                cp.start()
                cp.wait()

    def body(r, c):
        p = pos_ref[0, 0, r]
        _row_copy(row_ref.at[pl.ds(r, 1), :], xs_hbm.at[pl.ds(p, 1), :], sem).start()
        return c

    lax.fori_loop(0, tm, body, 0, unroll=8)
    _row_copy(row_ref, xs_hbm.at[pl.ds(0, tm), :], sem).wait()


def _permute(rows, pos, ends, cap):
    b, n_rows, _ = rows.shape
    tm = ROUTE_TM
    n_t = n_rows // tm
    rows = rows.reshape(b * n_rows, ROW_W)
    return pl.pallas_call(
        _permute_kernel,
        grid_spec=pltpu.PrefetchScalarGridSpec(
            num_scalar_prefetch=1,
            grid=(b, n_t),
            in_specs=[
                pl.BlockSpec((1, 1, tm), lambda bi, t, e: (bi * n_t + t, 0, 0), memory_space=pltpu.SMEM),
                pl.BlockSpec((tm, ROW_W), lambda bi, t, e: (bi * n_t + t, 0)),
            ],
            out_specs=pl.BlockSpec(memory_space=pl.ANY),
            scratch_shapes=[
                pltpu.VMEM((MOE_TMG, ROW_W), F32),
                pltpu.SemaphoreType.DMA(()),
                pltpu.SemaphoreType.DMA(()),
            ],
        ),
        out_shape=jax.ShapeDtypeStruct((cap, ROW_W), F32),
        name="moe_permute",
        compiler_params=_cparams(("arbitrary", "arbitrary")),
    )(ends, pos, rows)


def _ffn_kernel(grp_ref, val_ref, x_ref, w1_ref, w3_ref, w2_ref, o_ref):
    i = pl.program_id(0)

    @pl.when(val_ref[i] > 0)
    def _():
        x = x_ref[:, 0:D_MODEL].astype(BF16)
        gates = x_ref[:, D_MODEL:ROW_W]
        acc = None
        for k in range(EXPERTS_PER_GROUP):
            a = jnp.dot(x, w1_ref[k], preferred_element_type=F32)
            b = jnp.dot(x, w3_ref[k], preferred_element_type=F32)
            u = (a * (1.0 / (1.0 + jnp.exp(-a)))) * b * gates[:, k:k + 1]
            y = jnp.dot(u.astype(BF16), w2_ref[k], preferred_element_type=F32)
            acc = y if acc is None else acc + y
        o_ref[...] = acc

    @pl.when(val_ref[i] == 0)
    def _():
        o_ref[...] = jnp.zeros_like(o_ref)


def _ffn(xs_sorted, tile_grp, tile_valid, w1, w3, w2):
    cap = xs_sorted.shape[0]
    tm = MOE_TMG
    g4 = EXPERTS_PER_GROUP
    return pl.pallas_call(
        _ffn_kernel,
        grid_spec=pltpu.PrefetchScalarGridSpec(
            num_scalar_prefetch=2,
            grid=(cap // tm,),
            in_specs=[
                pl.BlockSpec((tm, ROW_W), lambda i, g, v: (i, 0)),
                pl.BlockSpec((g4, D_MODEL, D_EXPERT), lambda i, g, v: (g[i], 0, 0)),
                pl.BlockSpec((g4, D_MODEL, D_EXPERT), lambda i, g, v: (g[i], 0, 0)),
                pl.BlockSpec((g4, D_EXPERT, D_MODEL), lambda i, g, v: (g[i], 0, 0)),
            ],
            out_specs=pl.BlockSpec((tm, D_MODEL), lambda i, g, v: (i, 0)),
        ),
        out_shape=jax.ShapeDtypeStruct((cap, D_MODEL), F32),
        name="moe_ffn",
        compiler_params=_cparams(("arbitrary",)),
    )(tile_grp, tile_valid, xs_sorted, w1, w3, w2)


def _combine_kernel(pos_ref, x_ref, mod_ref, gfin_ref, ys_hbm, o_ref, buf_ref, sem, *, final):
    tm = x_ref.shape[0]

    def body(r, c):
        p = pos_ref[0, 0, r]
        _row_copy(ys_hbm.at[pl.ds(p, 1), :], buf_ref.at[pl.ds(r, 1), :], sem).start()
        return c

    lax.fori_loop(0, tm, body, 0, unroll=8)
    _row_copy(ys_hbm.at[pl.ds(0, tm), :], buf_ref, sem).wait()
    out = x_ref[...] + mod_ref[5:6, :] * buf_ref[...]
    if final:
        out = _rms(out) * gfin_ref[...]
    o_ref[...] = out


def _combine(xs, ys, pos, mod, gfin, n_lat, n_rows, final):
    b, s_tot, _ = xs.shape
    tm = ROUTE_TM
    n_t = n_rows // tm
    ctx_t0 = n_lat // tm
    x_spec = pl.BlockSpec((None, tm, D_MODEL), lambda bi, t: (bi, t, 0))
    return pl.pallas_call(
        functools.partial(_combine_kernel, final=final),
        grid=(b, n_t),
        in_specs=[
            pl.BlockSpec((1, 1, tm), lambda bi, t: (bi * n_t + t, 0, 0), memory_space=pltpu.SMEM),
            x_spec,
            pl.BlockSpec((None, 6, D_MODEL), lambda bi, t: (jnp.where(t >= ctx_t0, b, bi), 0, 0)),
            pl.BlockSpec((1, D_MODEL), lambda bi, t: (0, 0)),
            pl.BlockSpec(memory_space=pl.ANY),
        ],
        out_specs=x_spec,
        out_shape=jax.ShapeDtypeStruct((b, n_rows, D_MODEL) if final else xs.shape, F32),
        scratch_shapes=[pltpu.VMEM((tm, D_MODEL), F32), pltpu.SemaphoreType.DMA(())],
        input_output_aliases={} if final else {1: 0},
        name="moe_combine",
        compiler_params=_cparams(("arbitrary", "arbitrary")),
    )(pos, xs, mod, gfin, ys)


def _moe(xs, mod, gnorm, wr_t, br_col, w1, w3, w2, gfin, *, n_lat, n_rows, final):
    b = xs.shape[0]
    n_tok = b * n_rows
    rows, info, counts = _route(xs, mod, gnorm, wr_t, br_col, n_lat, n_rows)

    tmg = MOE_TMG
    cap = (n_tok // tmg + N_GROUPS) * tmg
    cnt = counts[:N_GROUPS, 0].astype(jnp.int32)
    padded = (cnt + tmg - 1) // tmg * tmg
    ends = jnp.cumsum(padded)
    starts = ends - padded
    gsel = info[:, 0, :].astype(jnp.int32)
    pos = (starts[gsel] + info[:, 1, :].astype(jnp.int32)).reshape(n_tok // ROUTE_TM, 1, ROUTE_TM)
    tile_start = jnp.arange(cap // tmg, dtype=jnp.int32) * tmg
    tile_valid = (tile_start < ends[-1]).astype(jnp.int32)
    tile_grp = jnp.minimum(jnp.searchsorted(ends, tile_start, side="right"), N_GROUPS - 1).astype(jnp.int32)
    last_grp = jnp.max(jnp.where(tile_valid > 0, tile_grp, 0))
    tile_grp = jnp.where(tile_valid > 0, tile_grp, last_grp)

    xs_sorted = _permute(rows, pos, ends.astype(jnp.int32), cap)
    ys = _ffn(xs_sorted, tile_grp, tile_valid, w1, w3, w2)
    return _combine(xs, ys, pos, mod, gfin, n_lat, n_rows, final)


def _rope_tables(n_lat, n_ctx):
    pos = jnp.arange(n_lat)
    row = (pos // GRID_W).astype(F32)
    col = (pos % GRID_W).astype(F32)
    n_freq = HEAD_DIM // 4
    inv = 1.0 / (ROPE_BASE ** (jnp.arange(n_freq, dtype=F32) / n_freq))
    lane = jnp.arange(LANES) % HEAD_DIM
    axis = lane // 32
    second = (lane % 32) // 16
    freq = inv[lane % 16]
    ang = jnp.where(axis[None, :] == 0, row[:, None], col[:, None]) * freq[None, :]
    cos = jnp.cos(ang)
    sin = jnp.sin(ang) * jnp.where(second == 0, -1.0, 1.0)[None, :]
    cos = jnp.concatenate([cos, jnp.ones((n_ctx, LANES), F32)], axis=0)
    sin = jnp.concatenate([sin, jnp.zeros((n_ctx, LANES), F32)], axis=0)
    q_scale = HEAD_DIM ** -0.5 * LOG2E
    return cos * q_scale, sin * q_scale, cos, sin


def _proj_cols():
    base = jnp.arange(D_IN)
    kb0 = base[2048:2112]
    kb1 = base[2112:2176]
    return jnp.concatenate([base[:2048], kb0, kb0, kb1, kb1, base[2176:]])


def kernel(x, c, ctx, c_ctx, w_ada, b_ada, norm_attn, norm_ffn, w_in, w_out, lambda_qk, subln, sink, w_router,
           b_router, w1, w3, w2, norm_final):
    b, n_lat, _ = x.shape
    n_ctx = ctx.shape[1]
    depth = w_in.shape[0]

    xs = jnp.concatenate([x, ctx], axis=1)
    c_all = jnp.zeros((MOD_ROWS, D_MODEL), F32).at[:b].set(c).at[b].set(c_ctx)
    mod_all = _ada(c_all, w_ada, b_ada).reshape(depth, MOD_ROWS, 6, D_MODEL)
    tables = _rope_tables(n_lat, n_ctx)
    cols = _proj_cols()

    wr_hi, wr_lo = _split_bf16(w_router.T)
    wr_t = jnp.concatenate([wr_hi, wr_lo], axis=0)
    br_col = b_router.reshape(N_EXPERTS, 1)
    gfin = norm_final.reshape(1, D_MODEL)

    out = None
    for l in range(depth):
        last = l == depth - 1
        lam_init = 0.8 - 0.6 * math.exp(-0.3 * l)
        mod = mod_all[l]
        w_in_l = w_in[l][:, cols].astype(BF16)
        proj = _proj(xs, mod, norm_attn[l].reshape(1, D_MODEL), w_in_l, tables, n_lat)
        oa = _attn_a(proj, lambda_qk[l], subln[l].reshape(2 * HEAD_DIM, 1), n_lat, not last, lam_init)
        ob = _attn_b(proj, sink[l], n_lat, not last)
        n_rows = n_lat if last else n_lat + n_ctx
        xs = _outproj(xs, oa, ob, w_out[l].astype(BF16), mod, n_lat, n_rows)
        gn = norm_ffn[l].reshape(1, D_MODEL)
        w1b, w3b, w2b = w1[l].astype(BF16), w3[l].astype(BF16), w2[l].astype(BF16)
        res = _moe(xs, mod, gn, wr_t, br_col, w1b, w3b, w2b, gfin, n_lat=n_lat, n_rows=n_rows, final=last)
        if last:
            out = res
        else:
            xs = res
    return out
```

```python
import functools
import math

import jax
import jax.numpy as jnp
from jax import lax
from jax.experimental import pallas as pl
from jax.experimental.pallas import tpu as pltpu

F32 = jnp.float32
BF16 = jnp.bfloat16

D_MODEL = 1024
HEAD_DIM = 64
HA = 4
HB = 8
KVB = 2
GB = HB // KVB
GRID_W = 64
WINDOW = 128
BLOCK = 128
ROPE_BASE = 10000.0
N_EXPERTS = 16
N_GROUPS = 4
EXPERTS_PER_GROUP = N_EXPERTS // N_GROUPS
D_EXPERT = 512
EPS = 1e-6
NEG = -1e30
LOG2E = 1.4426950408889634

LANES = 128
D_QA = HA * 2 * HEAD_DIM
D_IN = 2304
N_PROJ_TILES = 19
D_PROJ = N_PROJ_TILES * LANES
Q_TILES = (0, 1, 2, 3, 12, 13, 14, 15)
K_TILES = (4, 5, 6, 7, 16, 17)

VMEM_LIMIT = 56 * 1024 * 1024

PROJ_TM = 256
ATTN_TQ = 1024
ATTN_KC = 512
ROUTE_TM = 256
MOE_TMG = 512
OUT_SLAB = D_MODEL // LANES
IN_SLAB = 2 * OUT_SLAB
MOD_ROWS = 16


def _cparams(sem):
    return pltpu.CompilerParams(dimension_semantics=sem, vmem_limit_bytes=VMEM_LIMIT)


def _nt_dot(a, b):
    return lax.dot_general(a, b, (((1,), (1,)), ((), ())), preferred_element_type=F32)


def _split_bf16(x):
    hi = x.astype(BF16)
    lo = (x - hi.astype(F32)).astype(BF16)
    return hi, lo


def _rms(x):
    return x * lax.rsqrt(jnp.mean(x * x, axis=-1, keepdims=True) + EPS)


def _ada_kernel(c_ref, w_ref, b_ref, o_ref):
    c = c_ref[...]
    a = c * (1.0 / (1.0 + jnp.exp(-c)))
    a_hi, a_lo = _split_bf16(a)
    w_hi, w_lo = _split_bf16(w_ref[...])
    acc = jnp.dot(a_hi, w_hi, preferred_element_type=F32)
    acc += jnp.dot(a_hi, w_lo, preferred_element_type=F32)
    acc += jnp.dot(a_lo, w_hi, preferred_element_type=F32)
    o_ref[...] = acc + b_ref[...]


def _ada(c_all, w_ada, b_ada):
    depth = w_ada.shape[0]
    n_out = w_ada.shape[2]
    tn = D_MODEL
    return pl.pallas_call(
        _ada_kernel,
        grid=(depth, n_out // tn),
        in_specs=[
            pl.BlockSpec((MOD_ROWS, D_MODEL), lambda l, n: (0, 0)),
            pl.BlockSpec((None, D_MODEL, tn), lambda l, n: (l, 0, n)),
            pl.BlockSpec((None, 1, tn), lambda l, n: (l, 0, n)),
        ],
        out_specs=pl.BlockSpec((None, MOD_ROWS, tn), lambda l, n: (l, 0, n)),
        out_shape=jax.ShapeDtypeStruct((depth, MOD_ROWS, n_out), F32),
        name="ada_mod",
        compiler_params=_cparams(("parallel", "parallel")),
    )(c_all, w_ada, b_ada.reshape(depth, 1, n_out))


def _proj_kernel(x_ref, mod_ref, g_ref, w_ref, cq_ref, sq_ref, ck_ref, sk_ref, o_ref):
    x = x_ref[...]
    h = _rms(x) * g_ref[...] * (1.0 + mod_ref[1:2, :]) + mod_ref[0:1, :]
    acc = jnp.dot(h.astype(BF16), w_ref[...], preferred_element_type=F32)
    tm = x.shape[0]
    lane = lax.broadcasted_iota(jnp.int32, (tm, LANES), 1)
    first_half = (lane % 32) < 16
    for t in range(N_PROJ_TILES):
        a = acc[:, t * LANES:(t + 1) * LANES]
        if t in Q_TILES or t in K_TILES:
            cos, sin = (cq_ref, sq_ref) if t in Q_TILES else (ck_ref, sk_ref)
            partner = jnp.where(first_half, pltpu.roll(a, LANES - 16, 1), pltpu.roll(a, 16, 1))
            a = a * cos[...] + partner * sin[...]
        o_ref[:, t * LANES:(t + 1) * LANES] = a.astype(BF16)


def _proj(xs, mod, gnorm, w, tables, n_lat):
    b, s_tot, _ = xs.shape
    tm = PROJ_TM
    n_t = s_tot // tm
    ctx_t0 = n_lat // tm
    tab_spec = pl.BlockSpec((tm, LANES), lambda bi, t: (t, 0))
    return pl.pallas_call(
        _proj_kernel,
        grid=(b, n_t),
        in_specs=[
            pl.BlockSpec((None, tm, D_MODEL), lambda bi, t: (bi, t, 0)),
            pl.BlockSpec((None, 6, D_MODEL), lambda bi, t: (jnp.where(t >= ctx_t0, b, bi), 0, 0)),
            pl.BlockSpec((1, D_MODEL), lambda bi, t: (0, 0)),
            pl.BlockSpec((D_MODEL, D_PROJ), lambda bi, t: (0, 0)),
            tab_spec, tab_spec, tab_spec, tab_spec,
        ],
        out_specs=pl.BlockSpec((None, tm, D_PROJ), lambda bi, t: (bi, t, 0)),
        out_shape=jax.ShapeDtypeStruct((b, s_tot, D_PROJ), BF16),
        name="in_proj",
        compiler_params=_cparams(("parallel", "parallel")),
    )(xs, mod, gnorm, w, *tables)


def _transpose_into(vt_ref, v_ref, s_tot):
    for c in range(s_tot // LANES):
        blk = v_ref[c * LANES:(c + 1) * LANES, :].astype(F32)
        vt_ref[:, c * LANES:(c + 1) * LANES] = blk.T.astype(BF16)


def _attn_a_kernel(q_ref, k_ref, v_ref, lam_ref, subln_ref, o_ref, vt_ref, acc1_ref, acc2_ref,
                   s0_ref, s1_ref, sc_ref, *, n_lat, n_ctx, do_ctx, lam_init):
    s_tot = n_lat + n_ctx
    _transpose_into(vt_ref, v_ref, s_tot)

    lq = lam_ref[...]
    lam = (jnp.exp(jnp.sum(lq[0:1] * lq[1:2], axis=1, keepdims=True))
           - jnp.exp(jnp.sum(lq[2:3] * lq[3:4], axis=1, keepdims=True)) + lam_init)
    lane = lax.broadcasted_iota(jnp.int32, (1, LANES), 1)
    m_lo = (lane < HEAD_DIM).astype(BF16)
    m_hi = (lane >= HEAD_DIM).astype(BF16)
    subln = subln_ref[...]

    def finish(o1, l1, o2, l2):
        o = o1 * (1.0 / l1) - lam * (o2 * (1.0 / l2))
        o = o * lax.rsqrt(jnp.mean(o * o, axis=0, keepdims=True) + EPS)
        o = o * subln * (1.0 - lam_init)
        return o.T.astype(BF16)

    def one_shot(ks, vt, qm):
        s = _nt_dot(ks, qm)
        m = jnp.max(s, axis=0, keepdims=True)
        p = jnp.exp2(s - m)
        l = jnp.sum(p, axis=0, keepdims=True)
        return jnp.dot(vt, p.astype(BF16), preferred_element_type=F32), m, l

    tq, kc = ATTN_TQ, ATTN_KC
    n_kv = n_lat // kc

    def q_tile(i, carry):
        q0 = pl.multiple_of(i * tq, tq)
        qs = q_ref[pl.ds(q0, tq), :]
        qm = (qs * m_lo, qs * m_hi)

        def score(k0, size, dst_ref):
            ks = k_ref[pl.ds(k0, size), :]
            maxes = []
            for mp in range(2):
                s = _nt_dot(ks, qm[mp])
                dst_ref[mp] = s
                maxes.append(jnp.max(s, axis=0, keepdims=True))
            return tuple(maxes)

        def update(src_ref, smaxes, k0, size, c):
            vt = vt_ref[:, pl.ds(k0, size)]
            out = []
            for mp, acc_ref in ((0, acc1_ref), (1, acc2_ref)):
                s = src_ref[mp]
                smax = smaxes[mp]
                if c is None:
                    m_new = smax
                    p = jnp.exp2(s - m_new)
                    l_new = jnp.sum(p, axis=0, keepdims=True)
                    acc_ref[...] = jnp.dot(vt, p.astype(BF16), preferred_element_type=F32)
                else:
                    m, l = c[2 * mp], c[2 * mp + 1]
                    m_new = jnp.maximum(m, smax)
                    alpha = jnp.exp2(m - m_new)
                    p = jnp.exp2(s - m_new)
                    l_new = alpha * l + jnp.sum(p, axis=0, keepdims=True)
                    acc_ref[...] = alpha * acc_ref[...] + jnp.dot(vt, p.astype(BF16),
                                                                  preferred_element_type=F32)
                out += [m_new, l_new]
            return tuple(out)

        mx_c = score(n_lat, n_ctx, sc_ref)
        mx0 = score(0, kc, s0_ref)
        c = update(sc_ref, mx_c, n_lat, n_ctx, None)

        def pair(t, st):
            c, mx0 = st[:4], st[4:]
            ka = pl.multiple_of(2 * t * kc, kc)
            kb = pl.multiple_of(ka + kc, kc)
            mx1 = score(kb, kc, s1_ref)
            c = update(s0_ref, mx0, ka, kc, c)
            mx0 = score(pl.multiple_of(kb + kc, kc), kc, s0_ref)
            return update(s1_ref, mx1, kb, kc, c) + mx0

        st = lax.fori_loop(0, n_kv // 2 - 1, pair, c + mx0)
        c, mx0 = st[:4], st[4:]
        mx1 = score((n_kv - 1) * kc, kc, s1_ref)
        c = update(s0_ref, mx0, (n_kv - 2) * kc, kc, c)
        m1, l1, m2, l2 = update(s1_ref, mx1, (n_kv - 1) * kc, kc, c)
        o_ref[pl.ds(q0, tq), :] = finish(acc1_ref[...], l1, acc2_ref[...], l2)
        return carry

    lax.fori_loop(0, n_lat // tq, q_tile, 0)

    if do_ctx:
        qs = q_ref[n_lat:s_tot, :]
        kctx = k_ref[n_lat:s_tot, :]
        vctx = vt_ref[:, n_lat:s_tot]
        o1, _, l1 = one_shot(kctx, vctx, qs * m_lo)
        o2, _, l2 = one_shot(kctx, vctx, qs * m_hi)
        o_ref[n_lat:s_tot, :] = finish(o1, l1, o2, l2)
    else:
        o_ref[n_lat:s_tot, :] = jnp.zeros((n_ctx, LANES), BF16)


def _attn_a(proj, lam_qk, subln_col, n_lat, do_ctx, lam_init):
    b, s_tot, _ = proj.shape
    n_ctx = s_tot - n_lat
    kern = functools.partial(_attn_a_kernel, n_lat=n_lat, n_ctx=n_ctx, do_ctx=do_ctx, lam_init=lam_init)
    blk = lambda off: pl.BlockSpec((None, s_tot, LANES), lambda bi, h: (bi, 0, off + h))
    return pl.pallas_call(
        kern,
        grid=(b, HA),
        in_specs=[
            blk(0), blk(4), blk(8),
            pl.BlockSpec((4, HEAD_DIM), lambda bi, h: (0, 0)),
            pl.BlockSpec((2 * HEAD_DIM, 1), lambda bi, h: (0, 0)),
        ],
        out_specs=pl.BlockSpec((None, s_tot, LANES), lambda bi, h: (bi, 0, h)),
        out_shape=jax.ShapeDtypeStruct((b, s_tot, HA * LANES), BF16),
        scratch_shapes=[
            pltpu.VMEM((LANES, s_tot), BF16),
            pltpu.VMEM((LANES, ATTN_TQ), F32),
            pltpu.VMEM((LANES, ATTN_TQ), F32),
            pltpu.VMEM((2, ATTN_KC, ATTN_TQ), F32),
            pltpu.VMEM((2, ATTN_KC, ATTN_TQ), F32),
            pltpu.VMEM((2, n_ctx, ATTN_TQ), F32),
        ],
        name="attn_diff",
        compiler_params=_cparams(("parallel", "parallel")),
    )(proj, proj, proj, lam_qk, subln_col)


def _attn_b_kernel(sink_ref, q_ref, k_ref, v_ref, o_ref, vt_ref, sc0_ref, sw0_ref, sc1_ref, sw1_ref,
                   *, n_lat, n_ctx, do_ctx):
    s_tot = n_lat + n_ctx
    j = pl.program_id(1)
    _transpose_into(vt_ref, v_ref, s_tot)

    lane = lax.broadcasted_iota(jnp.int32, (1, LANES), 1)
    m_lo = (lane < HEAD_DIM).astype(BF16)
    m_hi = (lane >= HEAD_DIM).astype(BF16)
    nq = BLOCK
    win = 3 * BLOCK
    sink_row = jnp.concatenate(
        [jnp.full((1, nq), sink_ref[j * GB + g] * LOG2E, F32) for g in range(GB)], axis=1)
    rc = (lax.broadcasted_iota(jnp.int32, (win, nq), 0) - lax.broadcasted_iota(jnp.int32, (win, nq), 1))
    v0 = pl.multiple_of(j * HEAD_DIM, HEAD_DIM)

    def q_stack(q0):
        qa = q_ref[pl.ds(q0, nq), 0:LANES]
        qb = q_ref[pl.ds(q0, nq), LANES:2 * LANES]
        return jnp.concatenate([qa * m_lo, qa * m_hi, qb * m_lo, qb * m_hi], axis=0)

    def emit(q0, o):
        for pr in range(2):
            pair = jnp.concatenate([o[:, (2 * pr) * nq:(2 * pr + 1) * nq],
                                    o[:, (2 * pr + 1) * nq:(2 * pr + 2) * nq]], axis=0)
            o_ref[pl.ds(q0, nq), pr * LANES:(pr + 1) * LANES] = pair.T.astype(BF16)

    kctx = k_ref[n_lat:s_tot, :]
    vctx = vt_ref[pl.ds(v0, HEAD_DIM), n_lat:s_tot]

    def win_start(i):
        return pl.multiple_of(jnp.clip((i - 1) * nq, 0, n_lat - win), nq)

    def score(i, sc_ref, sw_ref):
        qs = q_stack(pl.multiple_of(i * nq, nq))
        start = win_start(i)
        sc_ref[...] = _nt_dot(kctx, qs)
        s_w = _nt_dot(k_ref[pl.ds(start, win), :], qs)
        rel = rc + (start - i * nq)
        ok = (rel <= WINDOW) & (rel >= -WINDOW)
        sw_ref[...] = jnp.where(jnp.concatenate([ok] * GB, axis=1), s_w, NEG)

    def attend(i, sc_ref, sw_ref):
        vw = vt_ref[pl.ds(v0, HEAD_DIM), pl.ds(win_start(i), win)]
        s_c = sc_ref[...]
        s_w = sw_ref[...]
        m = jnp.maximum(jnp.maximum(jnp.max(s_c, axis=0, keepdims=True),
                                    jnp.max(s_w, axis=0, keepdims=True)), sink_row)
        p_c = jnp.exp2(s_c - m)
        p_w = jnp.exp2(s_w - m)
        l = (jnp.sum(p_c, axis=0, keepdims=True) + jnp.sum(p_w, axis=0, keepdims=True)
             + jnp.exp2(sink_row - m))
        o = (jnp.dot(vctx, p_c.astype(BF16), preferred_element_type=F32)
             + jnp.dot(vw, p_w.astype(BF16), preferred_element_type=F32))
        emit(pl.multiple_of(i * nq, nq), o * (1.0 / l))

    n_blk = n_lat // nq
    score(0, sc0_ref, sw0_ref)

    def pair(t, carry):
        score(2 * t + 1, sc1_ref, sw1_ref)
        attend(2 * t, sc0_ref, sw0_ref)
        score(2 * t + 2, sc0_ref, sw0_ref)
        attend(2 * t + 1, sc1_ref, sw1_ref)
        return carry

    lax.fori_loop(0, n_blk // 2 - 1, pair, 0)
    score(n_blk - 1, sc1_ref, sw1_ref)
    attend(n_blk - 2, sc0_ref, sw0_ref)
    attend(n_blk - 1, sc1_ref, sw1_ref)

    for cb in range(n_ctx // nq):
        q0 = n_lat + cb * nq
        if do_ctx:
            qs = q_stack(q0)
            s_c = _nt_dot(kctx, qs)
            m = jnp.maximum(jnp.max(s_c, axis=0, keepdims=True), sink_row)
            p_c = jnp.exp2(s_c - m)
            l = jnp.sum(p_c, axis=0, keepdims=True) + jnp.exp2(sink_row - m)
            o = jnp.dot(vctx, p_c.astype(BF16), preferred_element_type=F32)
            emit(q0, o * (1.0 / l))
        else:
            o_ref[q0:q0 + nq, :] = jnp.zeros((nq, 2 * LANES), BF16)


def _attn_b(proj, sink, n_lat, do_ctx):
    b, s_tot, _ = proj.shape
    n_ctx = s_tot - n_lat
    kern = functools.partial(_attn_b_kernel, n_lat=n_lat, n_ctx=n_ctx, do_ctx=do_ctx)
    return pl.pallas_call(
        kern,
        grid_spec=pltpu.PrefetchScalarGridSpec(
            num_scalar_prefetch=1,
            grid=(b, KVB),
            in_specs=[
                pl.BlockSpec((None, s_tot, 2 * LANES), lambda bi, j, s: (bi, 0, 6 + j)),
                pl.BlockSpec((None, s_tot, LANES), lambda bi, j, s: (bi, 0, 16 + j)),
                pl.BlockSpec((None, s_tot, LANES), lambda bi, j, s: (bi, 0, 18)),
            ],
            out_specs=pl.BlockSpec((None, s_tot, 2 * LANES), lambda bi, j, s: (bi, 0, j)),
            scratch_shapes=[
                pltpu.VMEM((LANES, s_tot), BF16),
                pltpu.VMEM((n_ctx, GB * BLOCK), F32),
                pltpu.VMEM((3 * BLOCK, GB * BLOCK), F32),
                pltpu.VMEM((n_ctx, GB * BLOCK), F32),
                pltpu.VMEM((3 * BLOCK, GB * BLOCK), F32),
            ],
        ),
        out_shape=jax.ShapeDtypeStruct((b, s_tot, HB * HEAD_DIM), BF16),
        name="attn_win",
        compiler_params=_cparams(("parallel", "parallel")),
    )(sink, proj, proj, proj)


def _outproj_kernel(x_ref, oa_ref, ob_ref, w_ref, mod_ref, o_ref):
    acc = jnp.dot(oa_ref[...], w_ref[0:D_QA, :], preferred_element_type=F32)
    acc += jnp.dot(ob_ref[...], w_ref[D_QA:2 * D_QA, :], preferred_element_type=F32)
    o_ref[...] = x_ref[...] + mod_ref[2:3, :] * acc


def _outproj(xs, oa, ob, w, mod, n_lat, n_rows):
    b, s_tot, _ = xs.shape
    tm = PROJ_TM
    ctx_t0 = n_lat // tm
    return pl.pallas_call(
        _outproj_kernel,
        grid=(b, n_rows // tm),
        in_specs=[
            pl.BlockSpec((None, tm, D_MODEL), lambda bi, t: (bi, t, 0)),
            pl.BlockSpec((None, tm, D_QA), lambda bi, t: (bi, t, 0)),
            pl.BlockSpec((None, tm, D_QA), lambda bi, t: (bi, t, 0)),
            pl.BlockSpec((D_MODEL, D_MODEL), lambda bi, t: (0, 0)),
            pl.BlockSpec((None, 6, D_MODEL), lambda bi, t: (jnp.where(t >= ctx_t0, b, bi), 0, 0)),
        ],
        out_specs=pl.BlockSpec((None, tm, D_MODEL), lambda bi, t: (bi, t, 0)),
        out_shape=jax.ShapeDtypeStruct(xs.shape, F32),
        input_output_aliases={0: 0},
        name="out_proj",
        compiler_params=_cparams(("parallel", "parallel")),
    )(xs, oa, ob, w, mod)


def _route_t(logits_t, bias_col):
    s = 1.0 / (1.0 + jnp.exp(-logits_t))
    sb = s + bias_col
    s_rows = [s[e:e + 1, :] for e in range(N_EXPERTS)]
    sb_rows = [sb[e:e + 1, :] for e in range(N_EXPERTS)]
    best = None
    gsel = None
    for g in range(N_GROUPS):
        a, b, c, d = sb_rows[4 * g:4 * g + 4]
        hi1, lo1 = jnp.maximum(a, b), jnp.minimum(a, b)
        hi2, lo2 = jnp.maximum(c, d), jnp.minimum(c, d)
        gs = jnp.maximum(hi1, hi2) + jnp.maximum(jnp.minimum(hi1, hi2), jnp.maximum(lo1, lo2))
        if g == 0:
            best, gsel = gs, jnp.zeros_like(gs, dtype=jnp.int32)
        else:
            better = gs > best
            gsel = jnp.where(better, g, gsel)
            best = jnp.where(better, gs, best)
    masked = [jnp.where(gsel == (e // EXPERTS_PER_GROUP), sb_rows[e], -jnp.inf) for e in range(N_EXPERTS)]

    def argtop(vals):
        v, i = vals[0], jnp.zeros_like(gsel)
        for e in range(1, N_EXPERTS):
            better = vals[e] > v
            i = jnp.where(better, e, i)
            v = jnp.where(better, vals[e], v)
        return i

    i1 = argtop(masked)
    i2 = argtop([jnp.where(i1 == e, -jnp.inf, masked[e]) for e in range(N_EXPERTS)])
    w1 = sum(jnp.where(i1 == e, s_rows[e], 0.0) for e in range(N_EXPERTS))
    w2 = sum(jnp.where(i2 == e, s_rows[e], 0.0) for e in range(N_EXPERTS))
    inv = 1.0 / (w1 + w2)
    w1, w2 = w1 * inv, w2 * inv
    first = gsel * EXPERTS_PER_GROUP
    gates = [jnp.where(i1 == first + k, w1, 0.0) + jnp.where(i2 == first + k, w2, 0.0)
             for k in range(EXPERTS_PER_GROUP)]
    return gsel, gates


def _route_kernel(x_ref, mod_ref, g_ref, wr_ref, br_ref, tri_ref, slab_ref, info_ref, cnt_ref, base_ref):
    tm = x_ref.shape[0]

    @pl.when((pl.program_id(0) == 0) & (pl.program_id(1) == 0))
    def _():
        base_ref[...] = jnp.zeros_like(base_ref)

    x = x_ref[...]
    h = _rms(x) * g_ref[...] * (1.0 + mod_ref[4:5, :]) + mod_ref[3:4, :]
    h_hi, h_lo = _split_bf16(h)
    lt = _nt_dot(wr_ref[...], h_hi)
    lt2 = _nt_dot(wr_ref[0:N_EXPERTS, :], h_lo)
    logits_t = lt[0:N_EXPERTS] + lt[N_EXPERTS:2 * N_EXPERTS] + lt2
    gsel, gates = _route_t(logits_t, br_ref[...])

    row128 = lax.broadcasted_iota(jnp.int32, (LANES, tm), 0)
    gate_t = jnp.zeros((LANES, tm), F32)
    for k in range(EXPERTS_PER_GROUP):
        gate_t = jnp.where(row128 == k, gates[k], gate_t)
    for s in range(OUT_SLAB):
        slab_ref[pl.ds(s, tm, stride=IN_SLAB), :] = h[:, s * LANES:(s + 1) * LANES]
    slab_ref[pl.ds(OUT_SLAB, tm, stride=IN_SLAB), :] = gate_t.T
    for s in range(OUT_SLAB + 1, IN_SLAB):
        slab_ref[pl.ds(s, tm, stride=IN_SLAB), :] = jnp.zeros((tm, LANES), F32)

    row8 = lax.broadcasted_iota(jnp.int32, (8, tm), 0)
    member = row8 == gsel
    ranks = jnp.dot(member.astype(BF16), tri_ref[...], preferred_element_type=F32) + base_ref[:, 0:1]
    rank = jnp.sum(jnp.where(member, ranks, 0.0), axis=0, keepdims=True)
    info_ref[...] = jnp.where(row8 == 0, gsel.astype(F32), jnp.where(row8 == 1, rank, 0.0))
    base_ref[...] = base_ref[...] + jnp.sum(member.astype(F32), axis=1, keepdims=True)
    cnt_ref[...] = base_ref[...]


def _route(xs, mod, gnorm, wr_t, br_col, n_lat, n_rows):
    b, s_tot, _ = xs.shape
    tm = ROUTE_TM
    n_t = n_rows // tm
    ctx_t0 = n_lat // tm
    tri = (jnp.arange(tm)[:, None] < jnp.arange(tm)[None, :]).astype(BF16)
    const = lambda shape: pl.BlockSpec(shape, lambda bi, t: (0,) * len(shape))
    return pl.pallas_call(
        _route_kernel,
        grid=(b, n_t),
        in_specs=[
            pl.BlockSpec((None, tm, D_MODEL), lambda bi, t: (bi, t, 0)),
            pl.BlockSpec((None, 6, D_MODEL), lambda bi, t: (jnp.where(t >= ctx_t0, b, bi), 0, 0)),
            const((1, D_MODEL)),
            const((2 * N_EXPERTS, D_MODEL)),
            const((N_EXPERTS, 1)),
            const((tm, tm)),
        ],
        out_specs=[
            pl.BlockSpec((tm * IN_SLAB, LANES), lambda bi, t: (bi * n_t + t, 0)),
            pl.BlockSpec((None, 8, tm), lambda bi, t: (bi, 0, t)),
            const((8, LANES)),
        ],
        out_shape=[
            jax.ShapeDtypeStruct((b * n_t * tm * IN_SLAB, LANES), F32),
            jax.ShapeDtypeStruct((b, 8, n_t * tm), F32),
            jax.ShapeDtypeStruct((8, LANES), F32),
        ],
        scratch_shapes=[pltpu.VMEM((8, LANES), F32)],
        name="moe_route",
        compiler_params=_cparams(("arbitrary", "arbitrary")),
    )(xs, mod, gnorm, wr_t, br_col, tri)


def _row_copy(src, dst, sem):
    return pltpu.make_async_copy(src, dst, sem)


def _slab(ref, i, n):
    return ref.at[pl.ds(pl.multiple_of(i * n, n), n), :]


def _slab_rows(ref, s, tm, n):
    return ref[pl.ds(s, tm, stride=n), :]


def _permute_kernel(ends_ref, pos_ref, slab_ref, xs_hbm, zero_ref, sem, zsem):
    tm = slab_ref.shape[0] // IN_SLAB
    tile = MOE_TMG * IN_SLAB

    @pl.when((pl.program_id(0) == 0) & (pl.program_id(1) == 0))
    def _():
        zero_ref[...] = jnp.zeros_like(zero_ref)
        for g in range(N_GROUPS):
            lo = ends_ref[g] if g == 0 else ends_ref[g] - ends_ref[g - 1]

            @pl.when(lo > 0)
            def _():
                dst = xs_hbm.at[pl.ds(pl.multiple_of((ends_ref[g] - MOE_TMG) * IN_SLAB, tile), tile), :]
                cp = _row_copy(zero_ref, dst, zsem)
                cp.start()
                cp.wait()

        n_tiles = xs_hbm.shape[0] // tile
        for t0 in range(n_tiles - N_GROUPS, n_tiles):
            @pl.when(t0 * MOE_TMG >= ends_ref[N_GROUPS - 1])
            def _():
                cp = _row_copy(zero_ref, xs_hbm.at[pl.ds(t0 * tile, tile), :], zsem)
                cp.start()
                cp.wait()

    def body(r, c):
        _row_copy(_slab(slab_ref, r, IN_SLAB), _slab(xs_hbm, pos_ref[0, 0, r], IN_SLAB), sem).start()
        return c

    lax.fori_loop(0, tm, body, 0, unroll=8)
    _row_copy(slab_ref, xs_hbm.at[pl.ds(0, tm * IN_SLAB), :], sem).wait()


def _permute(slabs, pos, ends, cap, b, n_rows):
    tm = ROUTE_TM
    n_t = n_rows // tm
    return pl.pallas_call(
        _permute_kernel,
        grid_spec=pltpu.PrefetchScalarGridSpec(
            num_scalar_prefetch=1,
            grid=(b, n_t),
            in_specs=[
                pl.BlockSpec((1, 1, tm), lambda bi, t, e: (bi * n_t + t, 0, 0), memory_space=pltpu.SMEM),
                pl.BlockSpec((tm * IN_SLAB, LANES), lambda bi, t, e: (bi * n_t + t, 0)),
            ],
            out_specs=pl.BlockSpec(memory_space=pl.ANY),
            scratch_shapes=[
                pltpu.VMEM((MOE_TMG * IN_SLAB, LANES), F32),
                pltpu.SemaphoreType.DMA(()),
                pltpu.SemaphoreType.DMA(()),
            ],
        ),
        out_shape=jax.ShapeDtypeStruct((cap * IN_SLAB, LANES), F32),
        name="moe_permute",
        compiler_params=_cparams(("arbitrary", "arbitrary")),
    )(ends, pos, slabs)


def _ffn_kernel(grp_ref, val_ref, x_ref, w1_ref, w3_ref, w2_ref, o_ref):
    i = pl.program_id(0)
    tm = x_ref.shape[0] // IN_SLAB

    @pl.when(val_ref[i] > 0)
    def _():
        x = jnp.concatenate([_slab_rows(x_ref, s, tm, IN_SLAB).astype(BF16) for s in range(OUT_SLAB)],
                            axis=1)
        gates = _slab_rows(x_ref, OUT_SLAB, tm, IN_SLAB)
        acc = None
        for k in range(EXPERTS_PER_GROUP):
            a = jnp.dot(x, w1_ref[k], preferred_element_type=F32)
            b = jnp.dot(x, w3_ref[k], preferred_element_type=F32)
            u = (a * (1.0 / (1.0 + jnp.exp(-a)))) * b * gates[:, k:k + 1]
            y = jnp.dot(u.astype(BF16), w2_ref[k], preferred_element_type=F32)
            acc = y if acc is None else acc + y
        for s in range(OUT_SLAB):
            o_ref[pl.ds(s, tm, stride=OUT_SLAB), :] = acc[:, s * LANES:(s + 1) * LANES]

    @pl.when(val_ref[i] == 0)
    def _():
        o_ref[...] = jnp.zeros_like(o_ref)


def _ffn(xs_sorted, tile_grp, tile_valid, w1, w3, w2):
    n_tiles = xs_sorted.shape[0] // (MOE_TMG * IN_SLAB)
    g4 = EXPERTS_PER_GROUP
    return pl.pallas_call(
        _ffn_kernel,
        grid_spec=pltpu.PrefetchScalarGridSpec(
            num_scalar_prefetch=2,
            grid=(n_tiles,),
            in_specs=[
                pl.BlockSpec((MOE_TMG * IN_SLAB, LANES), lambda i, g, v: (i, 0)),
                pl.BlockSpec((g4, D_MODEL, D_EXPERT), lambda i, g, v: (g[i], 0, 0)),
                pl.BlockSpec((g4, D_MODEL, D_EXPERT), lambda i, g, v: (g[i], 0, 0)),
                pl.BlockSpec((g4, D_EXPERT, D_MODEL), lambda i, g, v: (g[i], 0, 0)),
            ],
            out_specs=pl.BlockSpec((MOE_TMG * OUT_SLAB, LANES), lambda i, g, v: (i, 0)),
        ),
        out_shape=jax.ShapeDtypeStruct((n_tiles * MOE_TMG * OUT_SLAB, LANES), F32),
        name="moe_ffn",
        compiler_params=_cparams(("arbitrary",)),
    )(tile_grp, tile_valid, xs_sorted, w1, w3, w2)


def _combine_kernel(pos_ref, x_ref, mod_ref, gfin_ref, ys_hbm, o_ref, buf_ref, sem, *, final):
    tm = x_ref.shape[0]

    def body(r, c):
        _row_copy(_slab(ys_hbm, pos_ref[0, 0, r], OUT_SLAB), _slab(buf_ref, r, OUT_SLAB), sem).start()
        return c

    lax.fori_loop(0, tm, body, 0, unroll=8)
    _row_copy(ys_hbm.at[pl.ds(0, tm * OUT_SLAB), :], buf_ref, sem).wait()
    y = jnp.concatenate([_slab_rows(buf_ref, s, tm, OUT_SLAB) for s in range(OUT_SLAB)], axis=1)
    out = x_ref[...] + mod_ref[5:6, :] * y
    if final:
        out = _rms(out) * gfin_ref[...]
    o_ref[...] = out


def _combine(xs, ys, pos, mod, gfin, n_lat, n_rows, final):
    b, s_tot, _ = xs.shape
    tm = ROUTE_TM
    n_t = n_rows // tm
    ctx_t0 = n_lat // tm
    x_spec = pl.BlockSpec((None, tm, D_MODEL), lambda bi, t: (bi, t, 0))
    return pl.pallas_call(
        functools.partial(_combine_kernel, final=final),
        grid=(b, n_t),
        in_specs=[
            pl.BlockSpec((1, 1, tm), lambda bi, t: (bi * n_t + t, 0, 0), memory_space=pltpu.SMEM),
            x_spec,
            pl.BlockSpec((None, 6, D_MODEL), lambda bi, t: (jnp.where(t >= ctx_t0, b, bi), 0, 0)),
            pl.BlockSpec((1, D_MODEL), lambda bi, t: (0, 0)),
            pl.BlockSpec(memory_space=pl.ANY),
        ],
        out_specs=x_spec,
        out_shape=jax.ShapeDtypeStruct((b, n_rows, D_MODEL) if final else xs.shape, F32),
        scratch_shapes=[pltpu.VMEM((tm * OUT_SLAB, LANES), F32), pltpu.SemaphoreType.DMA(())],
        input_output_aliases={} if final else {1: 0},
        name="moe_combine",
        compiler_params=_cparams(("arbitrary", "arbitrary")),
    )(pos, xs, mod, gfin, ys)


def _moe(xs, mod, gnorm, wr_t, br_col, w1, w3, w2, gfin, *, n_lat, n_rows, final):
    b = xs.shape[0]
    n_tok = b * n_rows
    rows, info, counts = _route(xs, mod, gnorm, wr_t, br_col, n_lat, n_rows)

    tmg = MOE_TMG
    cap = (n_tok // tmg + N_GROUPS) * tmg
    cnt = counts[:N_GROUPS, 0].astype(jnp.int32)
    padded = (cnt + tmg - 1) // tmg * tmg
    ends = jnp.cumsum(padded)
    starts = ends - padded
    gsel = info[:, 0, :].astype(jnp.int32)
    start_of = sum(jnp.where(gsel == g, starts[g], 0) for g in range(N_GROUPS))
    pos = (start_of + info[:, 1, :].astype(jnp.int32)).reshape(n_tok // ROUTE_TM, 1, ROUTE_TM)
    tile_start = jnp.arange(cap // tmg, dtype=jnp.int32) * tmg
    tile_valid = (tile_start < ends[-1]).astype(jnp.int32)
    tile_grp = sum((tile_start >= ends[g]).astype(jnp.int32) for g in range(N_GROUPS - 1))
    last_grp = jnp.max(jnp.where(tile_valid > 0, tile_grp, 0))
    tile_grp = jnp.where(tile_valid > 0, tile_grp, last_grp)

    xs_sorted = _permute(rows, pos, ends.astype(jnp.int32), cap, b, n_rows)
    ys = _ffn(xs_sorted, tile_grp, tile_valid, w1, w3, w2)
    return _combine(xs, ys, pos, mod, gfin, n_lat, n_rows, final)


def _rope_tables(n_lat, n_ctx):
    pos = jnp.arange(n_lat)
    row = (pos // GRID_W).astype(F32)
    col = (pos % GRID_W).astype(F32)
    n_freq = HEAD_DIM // 4
    inv = 1.0 / (ROPE_BASE ** (jnp.arange(n_freq, dtype=F32) / n_freq))
    lane = jnp.arange(LANES) % HEAD_DIM
    axis = lane // 32
    second = (lane % 32) // 16
    freq = inv[lane % 16]
    ang = jnp.where(axis[None, :] == 0, row[:, None], col[:, None]) * freq[None, :]
    cos = jnp.cos(ang)
    sin = jnp.sin(ang) * jnp.where(second == 0, -1.0, 1.0)[None, :]
    cos = jnp.concatenate([cos, jnp.ones((n_ctx, LANES), F32)], axis=0)
    sin = jnp.concatenate([sin, jnp.zeros((n_ctx, LANES), F32)], axis=0)
    q_scale = HEAD_DIM ** -0.5 * LOG2E
    return cos * q_scale, sin * q_scale, cos, sin


def _proj_cols():
    base = jnp.arange(D_IN)
    kb0 = base[2048:2112]
    kb1 = base[2112:2176]
    return jnp.concatenate([base[:2048], kb0, kb0, kb1, kb1, base[2176:]])


def kernel(x, c, ctx, c_ctx, w_ada, b_ada, norm_attn, norm_ffn, w_in, w_out, lambda_qk, subln, sink, w_router,
           b_router, w1, w3, w2, norm_final):
    b, n_lat, _ = x.shape
    n_ctx = ctx.shape[1]
    depth = w_in.shape[0]

    xs = jnp.concatenate([x, ctx], axis=1)
    c_all = jnp.zeros((MOD_ROWS, D_MODEL), F32).at[:b].set(c).at[b].set(c_ctx)
    mod_all = _ada(c_all, w_ada, b_ada).reshape(depth, MOD_ROWS, 6, D_MODEL)
    tables = _rope_tables(n_lat, n_ctx)
    cols = _proj_cols()

    wr_hi, wr_lo = _split_bf16(w_router.T)
    wr_t = jnp.concatenate([wr_hi, wr_lo], axis=0)
    br_col = b_router.reshape(N_EXPERTS, 1)
    gfin = norm_final.reshape(1, D_MODEL)

    out = None
    for l in range(depth):
        last = l == depth - 1
        lam_init = 0.8 - 0.6 * math.exp(-0.3 * l)
        mod = mod_all[l]
        w_in_l = w_in[l][:, cols].astype(BF16)
        proj = _proj(xs, mod, norm_attn[l].reshape(1, D_MODEL), w_in_l, tables, n_lat)
        oa = _attn_a(proj, lambda_qk[l], subln[l].reshape(2 * HEAD_DIM, 1), n_lat, not last, lam_init)
        ob = _attn_b(proj, sink[l], n_lat, not last)
        n_rows = n_lat if last else n_lat + n_ctx
        xs = _outproj(xs, oa, ob, w_out[l].astype(BF16), mod, n_lat, n_rows)
        gn = norm_ffn[l].reshape(1, D_MODEL)
        w1b, w3b, w2b = w1[l].astype(BF16), w3[l].astype(BF16), w2[l].astype(BF16)
        res = _moe(xs, mod, gn, wr_t, br_col, w1b, w3b, w2b, gfin, n_lat=n_lat, n_rows=n_rows, final=last)
        if last:
            out = res
        else:
            xs = res
    return out
```

```python
import functools
import math

import jax
import jax.numpy as jnp
from jax import lax
from jax.experimental import pallas as pl
from jax.experimental.pallas import tpu as pltpu

F32 = jnp.float32
BF16 = jnp.bfloat16

D_MODEL = 1024
HEAD_DIM = 64
HA = 4
HB = 8
KVB = 2
GB = HB // KVB
GRID_W = 64
WINDOW = 128
BLOCK = 128
ROPE_BASE = 10000.0
N_EXPERTS = 16
N_GROUPS = 4
EXPERTS_PER_GROUP = N_EXPERTS // N_GROUPS
D_EXPERT = 512
EPS = 1e-6
NEG = -1e30
LOG2E = 1.4426950408889634

LANES = 128
D_QA = HA * 2 * HEAD_DIM
D_IN = 2304
N_PROJ_TILES = 19
D_PROJ = N_PROJ_TILES * LANES
Q_TILES = (0, 1, 2, 3, 12, 13, 14, 15)
K_TILES = (4, 5, 6, 7, 16, 17)

VMEM_LIMIT = 56 * 1024 * 1024

PROJ_TM = 256
ATTN_TQ = 1024
ATTN_KC = 512
ROUTE_TM = 256
PAIRS_PER_GROUP = 6
N_BUCKETS = N_GROUPS * PAIRS_PER_GROUP
BUCKET_ROWS = 32
MOE_TMG = 256
PERMUTE_TMS = (1024, 512, 256)
COMBINE_TILES = 4
OUT_SLAB = D_MODEL // LANES
IN_SLAB = 2 * OUT_SLAB
MOD_ROWS = 16


def _cparams(sem):
    return pltpu.CompilerParams(dimension_semantics=sem, vmem_limit_bytes=VMEM_LIMIT)


def _nt_dot(a, b):
    return lax.dot_general(a, b, (((1,), (1,)), ((), ())), preferred_element_type=F32)


def _split_bf16(x):
    hi = x.astype(BF16)
    lo = (x - hi.astype(F32)).astype(BF16)
    return hi, lo


def _rms(x):
    return x * lax.rsqrt(jnp.mean(x * x, axis=-1, keepdims=True) + EPS)


def _ada_kernel(c_ref, w_ref, b_ref, o_ref):
    c = c_ref[...]
    a = c * (1.0 / (1.0 + jnp.exp(-c)))
    a_hi, a_lo = _split_bf16(a)
    w_hi, w_lo = _split_bf16(w_ref[...])
    acc = jnp.dot(a_hi, w_hi, preferred_element_type=F32)
    acc += jnp.dot(a_hi, w_lo, preferred_element_type=F32)
    acc += jnp.dot(a_lo, w_hi, preferred_element_type=F32)
    o_ref[...] = acc + b_ref[...]


def _ada(c_all, w_ada, b_ada):
    depth = w_ada.shape[0]
    n_out = w_ada.shape[2]
    tn = D_MODEL
    return pl.pallas_call(
        _ada_kernel,
        grid=(depth, n_out // tn),
        in_specs=[
            pl.BlockSpec((MOD_ROWS, D_MODEL), lambda l, n: (0, 0)),
            pl.BlockSpec((None, D_MODEL, tn), lambda l, n: (l, 0, n)),
            pl.BlockSpec((None, 1, tn), lambda l, n: (l, 0, n)),
        ],
        out_specs=pl.BlockSpec((None, MOD_ROWS, tn), lambda l, n: (l, 0, n)),
        out_shape=jax.ShapeDtypeStruct((depth, MOD_ROWS, n_out), F32),
        name="ada_mod",
        compiler_params=_cparams(("parallel", "parallel")),
    )(c_all, w_ada, b_ada.reshape(depth, 1, n_out))


def _proj_kernel(x_ref, mod_ref, g_ref, w_ref, cq_ref, sq_ref, ck_ref, sk_ref, o_ref):
    x = x_ref[...]
    h = _rms(x) * g_ref[...] * (1.0 + mod_ref[1:2, :]) + mod_ref[0:1, :]
    acc = jnp.dot(h.astype(BF16), w_ref[...], preferred_element_type=F32)
    tm = x.shape[0]
    lane = lax.broadcasted_iota(jnp.int32, (tm, LANES), 1)
    first_half = (lane % 32) < 16
    for t in range(N_PROJ_TILES):
        a = acc[:, t * LANES:(t + 1) * LANES]
        if t in Q_TILES or t in K_TILES:
            cos, sin = (cq_ref, sq_ref) if t in Q_TILES else (ck_ref, sk_ref)
            partner = jnp.where(first_half, pltpu.roll(a, LANES - 16, 1), pltpu.roll(a, 16, 1))
            a = a * cos[...] + partner * sin[...]
        o_ref[:, t * LANES:(t + 1) * LANES] = a.astype(BF16)


def _proj(xs, mod, gnorm, w, tables, n_lat):
    b, s_tot, _ = xs.shape
    tm = PROJ_TM
    n_t = s_tot // tm
    ctx_t0 = n_lat // tm
    tab_spec = pl.BlockSpec((tm, LANES), lambda bi, t: (t, 0))
    return pl.pallas_call(
        _proj_kernel,
        grid=(b, n_t),
        in_specs=[
            pl.BlockSpec((None, tm, D_MODEL), lambda bi, t: (bi, t, 0)),
            pl.BlockSpec((None, 6, D_MODEL), lambda bi, t: (jnp.where(t >= ctx_t0, b, bi), 0, 0)),
            pl.BlockSpec((1, D_MODEL), lambda bi, t: (0, 0)),
            pl.BlockSpec((D_MODEL, D_PROJ), lambda bi, t: (0, 0)),
            tab_spec, tab_spec, tab_spec, tab_spec,
        ],
        out_specs=pl.BlockSpec((None, tm, D_PROJ), lambda bi, t: (bi, t, 0)),
        out_shape=jax.ShapeDtypeStruct((b, s_tot, D_PROJ), BF16),
        name="in_proj",
        compiler_params=_cparams(("parallel", "parallel")),
    )(xs, mod, gnorm, w, *tables)


def _transpose_into(vt_ref, v_ref, s_tot):
    for c in range(s_tot // LANES):
        blk = v_ref[c * LANES:(c + 1) * LANES, :].astype(F32)
        vt_ref[:, c * LANES:(c + 1) * LANES] = blk.T.astype(BF16)


def _attn_a_kernel(q_ref, k_ref, v_ref, lam_ref, subln_ref, o_ref, vt_ref, acc1_ref, acc2_ref,
                   s0_ref, s1_ref, sc_ref, *, n_lat, n_ctx, do_ctx, lam_init):
    s_tot = n_lat + n_ctx
    _transpose_into(vt_ref, v_ref, s_tot)

    lq = lam_ref[...]
    lam = (jnp.exp(jnp.sum(lq[0:1] * lq[1:2], axis=1, keepdims=True))
           - jnp.exp(jnp.sum(lq[2:3] * lq[3:4], axis=1, keepdims=True)) + lam_init)
    lane = lax.broadcasted_iota(jnp.int32, (1, LANES), 1)
    m_lo = (lane < HEAD_DIM).astype(BF16)
    m_hi = (lane >= HEAD_DIM).astype(BF16)
    subln = subln_ref[...]

    def finish(o1, l1, o2, l2):
        o = o1 * (1.0 / l1) - lam * (o2 * (1.0 / l2))
        o = o * lax.rsqrt(jnp.mean(o * o, axis=0, keepdims=True) + EPS)
        o = o * subln * (1.0 - lam_init)
        return o.T.astype(BF16)

    def one_shot(ks, vt, qm):
        s = _nt_dot(ks, qm)
        m = jnp.max(s, axis=0, keepdims=True)
        p = jnp.exp2(s - m)
        l = jnp.sum(p, axis=0, keepdims=True)
        return jnp.dot(vt, p.astype(BF16), preferred_element_type=F32), m, l

    tq, kc = ATTN_TQ, ATTN_KC
    n_kv = n_lat // kc

    def q_tile(i, carry):
        q0 = pl.multiple_of(i * tq, tq)
        qs = q_ref[pl.ds(q0, tq), :]
        qm = (qs * m_lo, qs * m_hi)

        def score(k0, size, dst_ref):
            ks = k_ref[pl.ds(k0, size), :]
            maxes = []
            for mp in range(2):
                s = _nt_dot(ks, qm[mp])
                dst_ref[mp] = s
                maxes.append(jnp.max(s, axis=0, keepdims=True))
            return tuple(maxes)

        def update(src_ref, smaxes, k0, size, c):
            vt = vt_ref[:, pl.ds(k0, size)]
            out = []
            for mp, acc_ref in ((0, acc1_ref), (1, acc2_ref)):
                s = src_ref[mp]
                smax = smaxes[mp]
                if c is None:
                    m_new = smax
                    p = jnp.exp2(s - m_new)
                    l_new = jnp.sum(p, axis=0, keepdims=True)
                    acc_ref[...] = jnp.dot(vt, p.astype(BF16), preferred_element_type=F32)
                else:
                    m, l = c[2 * mp], c[2 * mp + 1]
                    m_new = jnp.maximum(m, smax)
                    alpha = jnp.exp2(m - m_new)
                    p = jnp.exp2(s - m_new)
                    l_new = alpha * l + jnp.sum(p, axis=0, keepdims=True)
                    acc_ref[...] = alpha * acc_ref[...] + jnp.dot(vt, p.astype(BF16),
                                                                  preferred_element_type=F32)
                out += [m_new, l_new]
            return tuple(out)

        mx_c = score(n_lat, n_ctx, sc_ref)
        mx0 = score(0, kc, s0_ref)
        c = update(sc_ref, mx_c, n_lat, n_ctx, None)

        def pair(t, st):
            c, mx0 = st[:4], st[4:]
            ka = pl.multiple_of(2 * t * kc, kc)
            kb = pl.multiple_of(ka + kc, kc)
            mx1 = score(kb, kc, s1_ref)
            c = update(s0_ref, mx0, ka, kc, c)
            mx0 = score(pl.multiple_of(kb + kc, kc), kc, s0_ref)
            return update(s1_ref, mx1, kb, kc, c) + mx0

        st = lax.fori_loop(0, n_kv // 2 - 1, pair, c + mx0)
        c, mx0 = st[:4], st[4:]
        mx1 = score((n_kv - 1) * kc, kc, s1_ref)
        c = update(s0_ref, mx0, (n_kv - 2) * kc, kc, c)
        m1, l1, m2, l2 = update(s1_ref, mx1, (n_kv - 1) * kc, kc, c)
        o_ref[pl.ds(q0, tq), :] = finish(acc1_ref[...], l1, acc2_ref[...], l2)
        return carry

    lax.fori_loop(0, n_lat // tq, q_tile, 0)

    if do_ctx:
        qs = q_ref[n_lat:s_tot, :]
        kctx = k_ref[n_lat:s_tot, :]
        vctx = vt_ref[:, n_lat:s_tot]
        o1, _, l1 = one_shot(kctx, vctx, qs * m_lo)
        o2, _, l2 = one_shot(kctx, vctx, qs * m_hi)
        o_ref[n_lat:s_tot, :] = finish(o1, l1, o2, l2)
    else:
        o_ref[n_lat:s_tot, :] = jnp.zeros((n_ctx, LANES), BF16)


def _attn_a(proj, lam_qk, subln_col, n_lat, do_ctx, lam_init):
    b, s_tot, _ = proj.shape
    n_ctx = s_tot - n_lat
    kern = functools.partial(_attn_a_kernel, n_lat=n_lat, n_ctx=n_ctx, do_ctx=do_ctx, lam_init=lam_init)
    blk = lambda off: pl.BlockSpec((None, s_tot, LANES), lambda bi, h: (bi, 0, off + h))
    return pl.pallas_call(
        kern,
        grid=(b, HA),
        in_specs=[
            blk(0), blk(4), blk(8),
            pl.BlockSpec((4, HEAD_DIM), lambda bi, h: (0, 0)),
            pl.BlockSpec((2 * HEAD_DIM, 1), lambda bi, h: (0, 0)),
        ],
        out_specs=pl.BlockSpec((None, s_tot, LANES), lambda bi, h: (bi, 0, h)),
        out_shape=jax.ShapeDtypeStruct((b, s_tot, HA * LANES), BF16),
        scratch_shapes=[
            pltpu.VMEM((LANES, s_tot), BF16),
            pltpu.VMEM((LANES, ATTN_TQ), F32),
            pltpu.VMEM((LANES, ATTN_TQ), F32),
            pltpu.VMEM((2, ATTN_KC, ATTN_TQ), F32),
            pltpu.VMEM((2, ATTN_KC, ATTN_TQ), F32),
            pltpu.VMEM((2, n_ctx, ATTN_TQ), F32),
        ],
        name="attn_diff",
        compiler_params=_cparams(("parallel", "parallel")),
    )(proj, proj, proj, lam_qk, subln_col)


def _attn_b_kernel(sink_ref, q_ref, k_ref, v_ref, o_ref, vt_ref, sc0_ref, sw0_ref, sc1_ref, sw1_ref,
                   *, n_lat, n_ctx, do_ctx):
    s_tot = n_lat + n_ctx
    j = pl.program_id(1)
    _transpose_into(vt_ref, v_ref, s_tot)

    lane = lax.broadcasted_iota(jnp.int32, (1, LANES), 1)
    m_lo = (lane < HEAD_DIM).astype(BF16)
    m_hi = (lane >= HEAD_DIM).astype(BF16)
    nq = BLOCK
    win = 3 * BLOCK
    sink_row = jnp.concatenate(
        [jnp.full((1, nq), sink_ref[j * GB + g] * LOG2E, F32) for g in range(GB)], axis=1)
    rc = (lax.broadcasted_iota(jnp.int32, (win, nq), 0) - lax.broadcasted_iota(jnp.int32, (win, nq), 1))
    v0 = pl.multiple_of(j * HEAD_DIM, HEAD_DIM)

    def q_stack(q0):
        qa = q_ref[pl.ds(q0, nq), 0:LANES]
        qb = q_ref[pl.ds(q0, nq), LANES:2 * LANES]
        return jnp.concatenate([qa * m_lo, qa * m_hi, qb * m_lo, qb * m_hi], axis=0)

    def emit(q0, o):
        for pr in range(2):
            pair = jnp.concatenate([o[:, (2 * pr) * nq:(2 * pr + 1) * nq],
                                    o[:, (2 * pr + 1) * nq:(2 * pr + 2) * nq]], axis=0)
            o_ref[pl.ds(q0, nq), pr * LANES:(pr + 1) * LANES] = pair.T.astype(BF16)

    kctx = k_ref[n_lat:s_tot, :]
    vctx = vt_ref[pl.ds(v0, HEAD_DIM), n_lat:s_tot]

    def win_start(i):
        return pl.multiple_of(jnp.clip((i - 1) * nq, 0, n_lat - win), nq)

    def score(i, sc_ref, sw_ref):
        qs = q_stack(pl.multiple_of(i * nq, nq))
        start = win_start(i)
        sc_ref[...] = _nt_dot(kctx, qs)
        s_w = _nt_dot(k_ref[pl.ds(start, win), :], qs)
        rel = rc + (start - i * nq)
        ok = (rel <= WINDOW) & (rel >= -WINDOW)
        sw_ref[...] = jnp.where(jnp.concatenate([ok] * GB, axis=1), s_w, NEG)

    def attend(i, sc_ref, sw_ref):
        vw = vt_ref[pl.ds(v0, HEAD_DIM), pl.ds(win_start(i), win)]
        s_c = sc_ref[...]
        s_w = sw_ref[...]
        m = jnp.maximum(jnp.maximum(jnp.max(s_c, axis=0, keepdims=True),
                                    jnp.max(s_w, axis=0, keepdims=True)), sink_row)
        p_c = jnp.exp2(s_c - m)
        p_w = jnp.exp2(s_w - m)
        l = (jnp.sum(p_c, axis=0, keepdims=True) + jnp.sum(p_w, axis=0, keepdims=True)
             + jnp.exp2(sink_row - m))
        o = (jnp.dot(vctx, p_c.astype(BF16), preferred_element_type=F32)
             + jnp.dot(vw, p_w.astype(BF16), preferred_element_type=F32))
        emit(pl.multiple_of(i * nq, nq), o * (1.0 / l))

    n_blk = n_lat // nq
    score(0, sc0_ref, sw0_ref)

    def pair(t, carry):
        score(2 * t + 1, sc1_ref, sw1_ref)
        attend(2 * t, sc0_ref, sw0_ref)
        score(2 * t + 2, sc0_ref, sw0_ref)
        attend(2 * t + 1, sc1_ref, sw1_ref)
        return carry

    lax.fori_loop(0, n_blk // 2 - 1, pair, 0)
    score(n_blk - 1, sc1_ref, sw1_ref)
    attend(n_blk - 2, sc0_ref, sw0_ref)
    attend(n_blk - 1, sc1_ref, sw1_ref)

    for cb in range(n_ctx // nq):
        q0 = n_lat + cb * nq
        if do_ctx:
            qs = q_stack(q0)
            s_c = _nt_dot(kctx, qs)
            m = jnp.maximum(jnp.max(s_c, axis=0, keepdims=True), sink_row)
            p_c = jnp.exp2(s_c - m)
            l = jnp.sum(p_c, axis=0, keepdims=True) + jnp.exp2(sink_row - m)
            o = jnp.dot(vctx, p_c.astype(BF16), preferred_element_type=F32)
            emit(q0, o * (1.0 / l))
        else:
            o_ref[q0:q0 + nq, :] = jnp.zeros((nq, 2 * LANES), BF16)


def _attn_b(proj, sink, n_lat, do_ctx):
    b, s_tot, _ = proj.shape
    n_ctx = s_tot - n_lat
    kern = functools.partial(_attn_b_kernel, n_lat=n_lat, n_ctx=n_ctx, do_ctx=do_ctx)
    return pl.pallas_call(
        kern,
        grid_spec=pltpu.PrefetchScalarGridSpec(
            num_scalar_prefetch=1,
            grid=(b, KVB),
            in_specs=[
                pl.BlockSpec((None, s_tot, 2 * LANES), lambda bi, j, s: (bi, 0, 6 + j)),
                pl.BlockSpec((None, s_tot, LANES), lambda bi, j, s: (bi, 0, 16 + j)),
                pl.BlockSpec((None, s_tot, LANES), lambda bi, j, s: (bi, 0, 18)),
            ],
            out_specs=pl.BlockSpec((None, s_tot, 2 * LANES), lambda bi, j, s: (bi, 0, j)),
            scratch_shapes=[
                pltpu.VMEM((LANES, s_tot), BF16),
                pltpu.VMEM((n_ctx, GB * BLOCK), F32),
                pltpu.VMEM((3 * BLOCK, GB * BLOCK), F32),
                pltpu.VMEM((n_ctx, GB * BLOCK), F32),
                pltpu.VMEM((3 * BLOCK, GB * BLOCK), F32),
            ],
        ),
        out_shape=jax.ShapeDtypeStruct((b, s_tot, HB * HEAD_DIM), BF16),
        name="attn_win",
        compiler_params=_cparams(("parallel", "parallel")),
    )(sink, proj, proj, proj)


def _outproj_kernel(x_ref, oa_ref, ob_ref, w_ref, mod_ref, o_ref):
    acc = jnp.dot(oa_ref[...], w_ref[0:D_QA, :], preferred_element_type=F32)
    acc += jnp.dot(ob_ref[...], w_ref[D_QA:2 * D_QA, :], preferred_element_type=F32)
    o_ref[...] = x_ref[...] + mod_ref[2:3, :] * acc


def _outproj(xs, oa, ob, w, mod, n_lat, n_rows):
    b, s_tot, _ = xs.shape
    tm = PROJ_TM
    ctx_t0 = n_lat // tm
    return pl.pallas_call(
        _outproj_kernel,
        grid=(b, n_rows // tm),
        in_specs=[
            pl.BlockSpec((None, tm, D_MODEL), lambda bi, t: (bi, t, 0)),
            pl.BlockSpec((None, tm, D_QA), lambda bi, t: (bi, t, 0)),
            pl.BlockSpec((None, tm, D_QA), lambda bi, t: (bi, t, 0)),
            pl.BlockSpec((D_MODEL, D_MODEL), lambda bi, t: (0, 0)),
            pl.BlockSpec((None, 6, D_MODEL), lambda bi, t: (jnp.where(t >= ctx_t0, b, bi), 0, 0)),
        ],
        out_specs=pl.BlockSpec((None, tm, D_MODEL), lambda bi, t: (bi, t, 0)),
        out_shape=jax.ShapeDtypeStruct(xs.shape, F32),
        input_output_aliases={0: 0},
        name="out_proj",
        compiler_params=_cparams(("parallel", "parallel")),
    )(xs, oa, ob, w, mod)


def _route_t(logits_t, bias_col):
    s = 1.0 / (1.0 + jnp.exp(-logits_t))
    sb = s + bias_col
    s_rows = [s[e:e + 1, :] for e in range(N_EXPERTS)]
    sb_rows = [sb[e:e + 1, :] for e in range(N_EXPERTS)]
    best = None
    gsel = None
    for g in range(N_GROUPS):
        a, b, c, d = sb_rows[4 * g:4 * g + 4]
        hi1, lo1 = jnp.maximum(a, b), jnp.minimum(a, b)
        hi2, lo2 = jnp.maximum(c, d), jnp.minimum(c, d)
        gs = jnp.maximum(hi1, hi2) + jnp.maximum(jnp.minimum(hi1, hi2), jnp.maximum(lo1, lo2))
        if g == 0:
            best, gsel = gs, jnp.zeros_like(gs, dtype=jnp.int32)
        else:
            better = gs > best
            gsel = jnp.where(better, g, gsel)
            best = jnp.where(better, gs, best)
    masked = [jnp.where(gsel == (e // EXPERTS_PER_GROUP), sb_rows[e], -jnp.inf) for e in range(N_EXPERTS)]

    def argtop(vals):
        v, i = vals[0], jnp.zeros_like(gsel)
        for e in range(1, N_EXPERTS):
            better = vals[e] > v
            i = jnp.where(better, e, i)
            v = jnp.where(better, vals[e], v)
        return i

    i1 = argtop(masked)
    i2 = argtop([jnp.where(i1 == e, -jnp.inf, masked[e]) for e in range(N_EXPERTS)])
    w1 = sum(jnp.where(i1 == e, s_rows[e], 0.0) for e in range(N_EXPERTS))
    w2 = sum(jnp.where(i2 == e, s_rows[e], 0.0) for e in range(N_EXPERTS))
    inv = 1.0 / (w1 + w2)
    w1, w2 = w1 * inv, w2 * inv
    swap = i2 < i1
    lo = jnp.where(swap, i2, i1) - gsel * EXPERTS_PER_GROUP
    hi = jnp.where(swap, i1, i2) - gsel * EXPERTS_PER_GROUP
    g_lo = jnp.where(swap, w2, w1)
    g_hi = jnp.where(swap, w1, w2)
    pair = jnp.where(lo == 0, hi - 1, jnp.where(lo == 1, hi + 1, PAIRS_PER_GROUP - 1))
    return gsel * PAIRS_PER_GROUP + pair, g_lo, g_hi


def _route_kernel(x_ref, mod_ref, g_ref, wr_ref, br_ref, tri_ref, slab_ref, info_ref, cnt_ref, base_ref):
    tm = x_ref.shape[0]

    @pl.when((pl.program_id(0) == 0) & (pl.program_id(1) == 0))
    def _():
        base_ref[...] = jnp.zeros_like(base_ref)

    x = x_ref[...]
    h = _rms(x) * g_ref[...] * (1.0 + mod_ref[4:5, :]) + mod_ref[3:4, :]
    h_hi, h_lo = _split_bf16(h)
    lt = _nt_dot(wr_ref[...], h_hi)
    lt2 = _nt_dot(wr_ref[0:N_EXPERTS, :], h_lo)
    logits_t = lt[0:N_EXPERTS] + lt[N_EXPERTS:2 * N_EXPERTS] + lt2
    bucket, g_lo, g_hi = _route_t(logits_t, br_ref[...])

    row128 = lax.broadcasted_iota(jnp.int32, (LANES, tm), 0)
    gate_t = jnp.where(row128 == 0, g_lo, jnp.where(row128 == 1, g_hi, 0.0))
    for s in range(OUT_SLAB):
        slab_ref[pl.ds(s, tm, stride=IN_SLAB), :] = h[:, s * LANES:(s + 1) * LANES]
    slab_ref[pl.ds(OUT_SLAB, tm, stride=IN_SLAB), :] = gate_t.T
    for s in range(OUT_SLAB + 1, IN_SLAB):
        slab_ref[pl.ds(s, tm, stride=IN_SLAB), :] = jnp.zeros((tm, LANES), F32)

    rowb = lax.broadcasted_iota(jnp.int32, (BUCKET_ROWS, tm), 0)
    member = rowb == bucket
    ranks = jnp.dot(member.astype(BF16), tri_ref[...], preferred_element_type=F32) + base_ref[:, 0:1]
    rank = jnp.sum(jnp.where(member, ranks, 0.0), axis=0, keepdims=True)
    row8 = lax.broadcasted_iota(jnp.int32, (8, tm), 0)
    info_ref[...] = jnp.where(row8 == 0, bucket.astype(F32), jnp.where(row8 == 1, rank, 0.0))
    base_ref[...] = base_ref[...] + jnp.sum(member.astype(F32), axis=1, keepdims=True)
    cnt_ref[...] = base_ref[...]


def _route(xs, mod, gnorm, wr_t, br_col, n_lat, n_rows):
    b, s_tot, _ = xs.shape
    tm = ROUTE_TM
    n_t = n_rows // tm
    ctx_t0 = n_lat // tm
    tri = (jnp.arange(tm)[:, None] < jnp.arange(tm)[None, :]).astype(BF16)
    const = lambda shape: pl.BlockSpec(shape, lambda bi, t: (0,) * len(shape))
    return pl.pallas_call(
        _route_kernel,
        grid=(b, n_t),
        in_specs=[
            pl.BlockSpec((None, tm, D_MODEL), lambda bi, t: (bi, t, 0)),
            pl.BlockSpec((None, 6, D_MODEL), lambda bi, t: (jnp.where(t >= ctx_t0, b, bi), 0, 0)),
            const((1, D_MODEL)),
            const((2 * N_EXPERTS, D_MODEL)),
            const((N_EXPERTS, 1)),
            const((tm, tm)),
        ],
        out_specs=[
            pl.BlockSpec((tm * IN_SLAB, LANES), lambda bi, t: (bi * n_t + t, 0)),
            pl.BlockSpec((None, 8, tm), lambda bi, t: (bi, 0, t)),
            const((BUCKET_ROWS, LANES)),
        ],
        out_shape=[
            jax.ShapeDtypeStruct((b * n_t * tm * IN_SLAB, LANES), F32),
            jax.ShapeDtypeStruct((b, 8, n_t * tm), F32),
            jax.ShapeDtypeStruct((BUCKET_ROWS, LANES), F32),
        ],
        scratch_shapes=[pltpu.VMEM((BUCKET_ROWS, LANES), F32)],
        name="moe_route",
        compiler_params=_cparams(("arbitrary", "arbitrary")),
    )(xs, mod, gnorm, wr_t, br_col, tri)


def _row_copy(src, dst, sem):
    return pltpu.make_async_copy(src, dst, sem)


def _slab(ref, i, n):
    return ref.at[pl.ds(pl.multiple_of(i * n, n), n), :]


def _slab_rows(ref, s, tm, n):
    return ref[pl.ds(s, tm, stride=n), :]


def _permute_kernel(ends_ref, pos_ref, slab_ref, xs_hbm, zero_ref, sem, zsem):
    tm = slab_ref.shape[0] // IN_SLAB
    tile = MOE_TMG * IN_SLAB

    @pl.when((pl.program_id(0) == 0) & (pl.program_id(1) == 0))
    def _():
        zero_ref[...] = jnp.zeros_like(zero_ref)

        def zero_tile(first_token):
            dst = xs_hbm.at[pl.ds(pl.multiple_of(first_token * IN_SLAB, tile), tile), :]
            cp = _row_copy(zero_ref, dst, zsem)
            cp.start()
            cp.wait()

        def pad_fill(g, c):
            prev = ends_ref[jnp.maximum(g - 1, 0)]
            size = ends_ref[g] - jnp.where(g == 0, 0, prev)

            @pl.when(size > 0)
            def _():
                zero_tile(ends_ref[g] - MOE_TMG)
            return c

        lax.fori_loop(0, N_BUCKETS, pad_fill, 0)

        n_tiles = xs_hbm.shape[0] // tile

        def tail_fill(t0, c):
            @pl.when(t0 * MOE_TMG >= ends_ref[N_BUCKETS - 1])
            def _():
                zero_tile(t0 * MOE_TMG)
            return c

        lax.fori_loop(n_tiles - N_BUCKETS, n_tiles, tail_fill, 0)

    def body(r, c):
        _row_copy(_slab(slab_ref, r, IN_SLAB), _slab(xs_hbm, pos_ref[0, 0, r], IN_SLAB), sem).start()
        return c

    lax.fori_loop(0, tm, body, 0, unroll=8)
    _row_copy(slab_ref, xs_hbm.at[pl.ds(0, tm * IN_SLAB), :], sem).wait()


def _permute(slabs, pos, ends, cap, tm):
    n_t = pos.shape[0]
    return pl.pallas_call(
        _permute_kernel,
        grid_spec=pltpu.PrefetchScalarGridSpec(
            num_scalar_prefetch=1,
            grid=(1, n_t),
            in_specs=[
                pl.BlockSpec((1, 1, tm), lambda bi, t, e: (t, 0, 0), memory_space=pltpu.SMEM),
                pl.BlockSpec((tm * IN_SLAB, LANES), lambda bi, t, e: (t, 0)),
            ],
            out_specs=pl.BlockSpec(memory_space=pl.ANY),
            scratch_shapes=[
                pltpu.VMEM((MOE_TMG * IN_SLAB, LANES), F32),
                pltpu.SemaphoreType.DMA(()),
                pltpu.SemaphoreType.DMA(()),
            ],
        ),
        out_shape=jax.ShapeDtypeStruct((cap * IN_SLAB, LANES), F32),
        name="moe_permute",
        compiler_params=_cparams(("arbitrary", "arbitrary")),
    )(ends, pos, slabs)


def _ffn_kernel(ea_ref, eb_ref, val_ref, x_ref, w1a_ref, w3a_ref, w2a_ref, w1b_ref, w3b_ref, w2b_ref, o_ref):
    i = pl.program_id(0)
    tm = x_ref.shape[0] // IN_SLAB

    @pl.when(val_ref[i] > 0)
    def _():
        x = jnp.concatenate([_slab_rows(x_ref, s, tm, IN_SLAB).astype(BF16) for s in range(OUT_SLAB)],
                            axis=1)
        gates = _slab_rows(x_ref, OUT_SLAB, tm, IN_SLAB)
        acc = None
        for k, (w1_ref, w3_ref, w2_ref) in enumerate(((w1a_ref, w3a_ref, w2a_ref), (w1b_ref, w3b_ref, w2b_ref))):
            a = jnp.dot(x, w1_ref[...], preferred_element_type=F32)
            b = jnp.dot(x, w3_ref[...], preferred_element_type=F32)
            u = (a * (1.0 / (1.0 + jnp.exp(-a)))) * b * gates[:, k:k + 1]
            y = jnp.dot(u.astype(BF16), w2_ref[...], preferred_element_type=F32)
            acc = y if acc is None else acc + y
        for s in range(OUT_SLAB):
            o_ref[pl.ds(s, tm, stride=OUT_SLAB), :] = acc[:, s * LANES:(s + 1) * LANES]

    @pl.when(val_ref[i] == 0)
    def _():
        o_ref[...] = jnp.zeros_like(o_ref)


def _ffn(xs_sorted, tile_ea, tile_eb, tile_valid, w1, w3, w2):
    n_tiles = xs_sorted.shape[0] // (MOE_TMG * IN_SLAB)
    up = lambda sel: pl.BlockSpec((None, D_MODEL, D_EXPERT), lambda i, ea, eb, v: (sel(ea, eb)[i], 0, 0))
    down = lambda sel: pl.BlockSpec((None, D_EXPERT, D_MODEL), lambda i, ea, eb, v: (sel(ea, eb)[i], 0, 0))
    first = lambda ea, eb: ea
    second = lambda ea, eb: eb
    return pl.pallas_call(
        _ffn_kernel,
        grid_spec=pltpu.PrefetchScalarGridSpec(
            num_scalar_prefetch=3,
            grid=(n_tiles,),
            in_specs=[
                pl.BlockSpec((MOE_TMG * IN_SLAB, LANES), lambda i, ea, eb, v: (i, 0)),
                up(first), up(first), down(first),
                up(second), up(second), down(second),
            ],
            out_specs=pl.BlockSpec((MOE_TMG * OUT_SLAB, LANES), lambda i, ea, eb, v: (i, 0)),
        ),
        out_shape=jax.ShapeDtypeStruct((n_tiles * MOE_TMG * OUT_SLAB, LANES), F32),
        name="moe_ffn",
        compiler_params=_cparams(("arbitrary",)),
    )(tile_ea, tile_eb, tile_valid, xs_sorted, w1, w3, w2, w1, w3, w2)


def _combine_kernel(pos_ref, x_ref, mod_ref, modc_ref, gfin_ref, ys_hbm, o_ref, buf_ref, sem, *, n_lat, final):
    tm = x_ref.shape[0]

    def body(r, c):
        _row_copy(_slab(ys_hbm, pos_ref[0, 0, r], OUT_SLAB), _slab(buf_ref, r, OUT_SLAB), sem).start()
        return c

    lax.fori_loop(0, tm, body, 0, unroll=8)
    _row_copy(ys_hbm.at[pl.ds(0, tm * OUT_SLAB), :], buf_ref, sem).wait()
    y = jnp.concatenate([_slab_rows(buf_ref, s, tm, OUT_SLAB) for s in range(OUT_SLAB)], axis=1)
    row = pl.program_id(1) * tm + lax.broadcasted_iota(jnp.int32, (tm, 1), 0)
    gate = jnp.where(row >= n_lat, modc_ref[5:6, :], mod_ref[5:6, :])
    out = x_ref[...] + gate * y
    if final:
        out = _rms(out) * gfin_ref[...]
    o_ref[...] = out


def _combine(xs, ys, pos, mod, gfin, n_lat, n_rows, tm, final):
    b, s_tot, _ = xs.shape
    n_t = n_rows // tm
    x_spec = pl.BlockSpec((None, tm, D_MODEL), lambda bi, t: (bi, t, 0))
    return pl.pallas_call(
        functools.partial(_combine_kernel, n_lat=n_lat, final=final),
        grid=(b, n_t),
        in_specs=[
            pl.BlockSpec((1, 1, tm), lambda bi, t: (bi * n_t + t, 0, 0), memory_space=pltpu.SMEM),
            x_spec,
            pl.BlockSpec((None, 6, D_MODEL), lambda bi, t: (bi, 0, 0)),
            pl.BlockSpec((None, 6, D_MODEL), lambda bi, t: (b, 0, 0)),
            pl.BlockSpec((1, D_MODEL), lambda bi, t: (0, 0)),
            pl.BlockSpec(memory_space=pl.ANY),
        ],
        out_specs=x_spec,
        out_shape=jax.ShapeDtypeStruct((b, n_rows, D_MODEL) if final else xs.shape, F32),
        scratch_shapes=[pltpu.VMEM((tm * OUT_SLAB, LANES), F32), pltpu.SemaphoreType.DMA(())],
        input_output_aliases={} if final else {1: 0},
        name="moe_combine",
        compiler_params=_cparams(("arbitrary", "arbitrary")),
    )(pos, xs, mod, mod, gfin, ys)


def _moe(xs, mod, gnorm, wr_t, br_col, w1, w3, w2, gfin, *, n_lat, n_rows, final):
    b = xs.shape[0]
    n_tok = b * n_rows
    rows, info, counts = _route(xs, mod, gnorm, wr_t, br_col, n_lat, n_rows)

    tmg = MOE_TMG
    cap = (n_tok // tmg + N_BUCKETS) * tmg
    cnt = counts[:N_BUCKETS, 0].astype(jnp.int32)
    padded = (cnt + tmg - 1) // tmg * tmg
    ends = jnp.cumsum(padded)
    starts = ends - padded
    bucket = info[:, 0, :].astype(jnp.int32)
    start_of = sum(jnp.where(bucket == k, starts[k], 0) for k in range(N_BUCKETS))
    pos = start_of + info[:, 1, :].astype(jnp.int32)
    tm_p = next(t for t in PERMUTE_TMS if n_tok % t == 0)
    tm_c = n_rows // COMBINE_TILES
    tile_start = jnp.arange(cap // tmg, dtype=jnp.int32) * tmg
    tile_valid = (tile_start < ends[-1]).astype(jnp.int32)
    tile_bkt = sum((tile_start >= ends[k]).astype(jnp.int32) for k in range(N_BUCKETS - 1))
    last_bkt = jnp.max(jnp.where(tile_valid > 0, tile_bkt, 0))
    tile_bkt = jnp.where(tile_valid > 0, tile_bkt, last_bkt)
    pair_lo = (0, 0, 0, 1, 1, 2)
    pair_hi = (1, 2, 3, 2, 3, 3)
    grp, pair = tile_bkt // PAIRS_PER_GROUP, tile_bkt % PAIRS_PER_GROUP
    tile_ea = grp * EXPERTS_PER_GROUP + sum(jnp.where(pair == p, pair_lo[p], 0) for p in range(PAIRS_PER_GROUP))
    tile_eb = grp * EXPERTS_PER_GROUP + sum(jnp.where(pair == p, pair_hi[p], 0) for p in range(PAIRS_PER_GROUP))

    xs_sorted = _permute(rows, pos.reshape(n_tok // tm_p, 1, tm_p), ends.astype(jnp.int32), cap, tm_p)
    ys = _ffn(xs_sorted, tile_ea.astype(jnp.int32), tile_eb.astype(jnp.int32), tile_valid, w1, w3, w2)
    return _combine(xs, ys, pos.reshape(n_tok // tm_c, 1, tm_c), mod, gfin, n_lat, n_rows, tm_c, final)


def _rope_tables(n_lat, n_ctx):
    pos = jnp.arange(n_lat)
    row = (pos // GRID_W).astype(F32)
    col = (pos % GRID_W).astype(F32)
    n_freq = HEAD_DIM // 4
    inv = 1.0 / (ROPE_BASE ** (jnp.arange(n_freq, dtype=F32) / n_freq))
    lane = jnp.arange(LANES) % HEAD_DIM
    axis = lane // 32
    second = (lane % 32) // 16
    freq = inv[lane % 16]
    ang = jnp.where(axis[None, :] == 0, row[:, None], col[:, None]) * freq[None, :]
    cos = jnp.cos(ang)
    sin = jnp.sin(ang) * jnp.where(second == 0, -1.0, 1.0)[None, :]
    cos = jnp.concatenate([cos, jnp.ones((n_ctx, LANES), F32)], axis=0)
    sin = jnp.concatenate([sin, jnp.zeros((n_ctx, LANES), F32)], axis=0)
    q_scale = HEAD_DIM ** -0.5 * LOG2E
    return cos * q_scale, sin * q_scale, cos, sin


def _proj_cols():
    base = jnp.arange(D_IN)
    kb0 = base[2048:2112]
    kb1 = base[2112:2176]
    return jnp.concatenate([base[:2048], kb0, kb0, kb1, kb1, base[2176:]])


def kernel(x, c, ctx, c_ctx, w_ada, b_ada, norm_attn, norm_ffn, w_in, w_out, lambda_qk, subln, sink, w_router,
           b_router, w1, w3, w2, norm_final):
    b, n_lat, _ = x.shape
    n_ctx = ctx.shape[1]
    depth = w_in.shape[0]

    xs = jnp.concatenate([x, ctx], axis=1)
    c_all = jnp.zeros((MOD_ROWS, D_MODEL), F32).at[:b].set(c).at[b].set(c_ctx)
    mod_all = _ada(c_all, w_ada, b_ada).reshape(depth, MOD_ROWS, 6, D_MODEL)
    tables = _rope_tables(n_lat, n_ctx)
    cols = _proj_cols()

    wr_hi, wr_lo = _split_bf16(w_router.T)
    wr_t = jnp.concatenate([wr_hi, wr_lo], axis=0)
    br_col = b_router.reshape(N_EXPERTS, 1)
    gfin = norm_final.reshape(1, D_MODEL)

    out = None
    for l in range(depth):
        last = l == depth - 1
        lam_init = 0.8 - 0.6 * math.exp(-0.3 * l)
        mod = mod_all[l]
        w_in_l = w_in[l][:, cols].astype(BF16)
        proj = _proj(xs, mod, norm_attn[l].reshape(1, D_MODEL), w_in_l, tables, n_lat)
        oa = _attn_a(proj, lambda_qk[l], subln[l].reshape(2 * HEAD_DIM, 1), n_lat, not last, lam_init)
        ob = _attn_b(proj, sink[l], n_lat, not last)
        n_rows = n_lat if last else n_lat + n_ctx
        xs = _outproj(xs, oa, ob, w_out[l].astype(BF16), mod, n_lat, n_rows)
        gn = norm_ffn[l].reshape(1, D_MODEL)
        w1b, w3b, w2b = w1[l].astype(BF16), w3[l].astype(BF16), w2[l].astype(BF16)
        res = _moe(xs, mod, gn, wr_t, br_col, w1b, w3b, w2b, gfin, n_lat=n_lat, n_rows=n_rows, final=last)
        if last:
            out = res
        else:
            xs = res
    return out
```

```python
import functools
import math

import jax
import jax.numpy as jnp
from jax import lax
from jax.experimental import pallas as pl
from jax.experimental.pallas import tpu as pltpu

F32 = jnp.float32
BF16 = jnp.bfloat16

D_MODEL = 1024
HEAD_DIM = 64
HA = 4
HB = 8
KVB = 2
GB = HB // KVB
GRID_W = 64
WINDOW = 128
BLOCK = 128
ROPE_BASE = 10000.0
N_EXPERTS = 16
N_GROUPS = 4
EXPERTS_PER_GROUP = N_EXPERTS // N_GROUPS
D_EXPERT = 512
EPS = 1e-6
NEG = -1e30
LOG2E = 1.4426950408889634

LANES = 128
D_QA = HA * 2 * HEAD_DIM
D_IN = 2304
N_PROJ_TILES = 19
D_PROJ = N_PROJ_TILES * LANES
Q_TILES = (0, 1, 2, 3, 12, 13, 14, 15)
K_TILES = (4, 5, 6, 7, 16, 17)

VMEM_LIMIT = 56 * 1024 * 1024

PROJ_TM = 256
ATTN_TQ = 1024
ATTN_KC = 512
ROUTE_TM = 256
ROUTE_SAMPLES = 2
PAIRS_PER_GROUP = 6
N_BUCKETS = N_GROUPS * PAIRS_PER_GROUP
BUCKET_ROWS = 32
MOE_TMG = 256
PERMUTE_TMS = (1024, 512, 256)
COMBINE_TILES = 4
OUT_SLAB = D_MODEL // LANES
IN_SLAB = 2 * OUT_SLAB
MOD_ROWS = 16


def _cparams(sem):
    return pltpu.CompilerParams(dimension_semantics=sem, vmem_limit_bytes=VMEM_LIMIT)


def _nt_dot(a, b):
    return lax.dot_general(a, b, (((1,), (1,)), ((), ())), preferred_element_type=F32)


def _split_bf16(x):
    hi = x.astype(BF16)
    lo = (x - hi.astype(F32)).astype(BF16)
    return hi, lo


def _rms(x):
    return x * lax.rsqrt(jnp.mean(x * x, axis=-1, keepdims=True) + EPS)


def _ada_kernel(c_ref, w_ref, b_ref, o_ref):
    c = c_ref[...]
    a = c * (1.0 / (1.0 + jnp.exp(-c)))
    a_hi, a_lo = _split_bf16(a)
    w_hi, w_lo = _split_bf16(w_ref[...])
    acc = jnp.dot(a_hi, w_hi, preferred_element_type=F32)
    acc += jnp.dot(a_hi, w_lo, preferred_element_type=F32)
    acc += jnp.dot(a_lo, w_hi, preferred_element_type=F32)
    o_ref[...] = acc + b_ref[...]


def _ada(c_all, w_ada, b_ada):
    depth = w_ada.shape[0]
    n_out = w_ada.shape[2]
    tn = D_MODEL
    return pl.pallas_call(
        _ada_kernel,
        grid=(depth, n_out // tn),
        in_specs=[
            pl.BlockSpec((MOD_ROWS, D_MODEL), lambda l, n: (0, 0)),
            pl.BlockSpec((None, D_MODEL, tn), lambda l, n: (l, 0, n)),
            pl.BlockSpec((None, 1, tn), lambda l, n: (l, 0, n)),
        ],
        out_specs=pl.BlockSpec((None, MOD_ROWS, tn), lambda l, n: (l, 0, n)),
        out_shape=jax.ShapeDtypeStruct((depth, MOD_ROWS, n_out), F32),
        name="ada_mod",
        compiler_params=_cparams(("parallel", "parallel")),
    )(c_all, w_ada, b_ada.reshape(depth, 1, n_out))


def _token_specs(tm, ctx_t0, ctx_off, samples=None):
    lat = pl.BlockSpec((samples, tm, D_MODEL), lambda bi, t, *_: (bi, jnp.minimum(t, ctx_t0 - 1), 0))
    ctx = pl.BlockSpec((samples, tm, D_MODEL), lambda bi, t, *_: (bi, ctx_off + jnp.maximum(t - ctx_t0, 0), 0))
    return lat, ctx


def _proj_kernel(lat_ref, ctx_ref, mod_ref, g_ref, w_ref, cq_ref, sq_ref, ck_ref, sk_ref, o_ref, *, ctx_t0):
    x = jnp.where(pl.program_id(1) >= ctx_t0, ctx_ref[...], lat_ref[...])
    h = _rms(x) * g_ref[...] * (1.0 + mod_ref[1:2, :]) + mod_ref[0:1, :]
    acc = jnp.dot(h.astype(BF16), w_ref[...], preferred_element_type=F32)
    tm = x.shape[0]
    lane = lax.broadcasted_iota(jnp.int32, (tm, LANES), 1)
    first_half = (lane % 32) < 16
    for t in range(N_PROJ_TILES):
        a = acc[:, t * LANES:(t + 1) * LANES]
        if t in Q_TILES or t in K_TILES:
            cos, sin = (cq_ref, sq_ref) if t in Q_TILES else (ck_ref, sk_ref)
            partner = jnp.where(first_half, pltpu.roll(a, LANES - 16, 1), pltpu.roll(a, 16, 1))
            a = a * cos[...] + partner * sin[...]
        o_ref[:, t * LANES:(t + 1) * LANES] = a.astype(BF16)


def _proj(lat_src, ctx_src, ctx_off, mod, gnorm, w, tables, n_lat, s_tot):
    b = lat_src.shape[0]
    tm = PROJ_TM
    n_t = s_tot // tm
    ctx_t0 = n_lat // tm
    tab_spec = pl.BlockSpec((tm, LANES), lambda bi, t: (t, 0))
    return pl.pallas_call(
        functools.partial(_proj_kernel, ctx_t0=ctx_t0),
        grid=(b, n_t),
        in_specs=[
            *_token_specs(tm, ctx_t0, ctx_off),
            pl.BlockSpec((None, 6, D_MODEL), lambda bi, t: (jnp.where(t >= ctx_t0, b, bi), 0, 0)),
            pl.BlockSpec((1, D_MODEL), lambda bi, t: (0, 0)),
            pl.BlockSpec((D_MODEL, D_PROJ), lambda bi, t: (0, 0)),
            tab_spec, tab_spec, tab_spec, tab_spec,
        ],
        out_specs=pl.BlockSpec((None, tm, D_PROJ), lambda bi, t: (bi, t, 0)),
        out_shape=jax.ShapeDtypeStruct((b, s_tot, D_PROJ), BF16),
        name="in_proj",
        compiler_params=_cparams(("parallel", "parallel")),
    )(lat_src, ctx_src, mod, gnorm, w, *tables)


def _transpose_into(vt_ref, v_ref, s_tot):
    for c in range(s_tot // LANES):
        blk = v_ref[c * LANES:(c + 1) * LANES, :].astype(F32)
        vt_ref[:, c * LANES:(c + 1) * LANES] = blk.T.astype(BF16)


def _attn_a_kernel(q_ref, k_ref, v_ref, lam_ref, subln_ref, o_ref, vt_ref, acc1_ref, acc2_ref,
                   s0_ref, s1_ref, sc_ref, *, n_lat, n_ctx, do_ctx, lam_init):
    s_tot = n_lat + n_ctx
    _transpose_into(vt_ref, v_ref, s_tot)

    lq = lam_ref[...]
    lam = (jnp.exp(jnp.sum(lq[0:1] * lq[1:2], axis=1, keepdims=True))
           - jnp.exp(jnp.sum(lq[2:3] * lq[3:4], axis=1, keepdims=True)) + lam_init)
    lane = lax.broadcasted_iota(jnp.int32, (1, LANES), 1)
    m_lo = (lane < HEAD_DIM).astype(BF16)
    m_hi = (lane >= HEAD_DIM).astype(BF16)
    subln = subln_ref[...]

    def finish(o1, l1, o2, l2):
        o = o1 * (1.0 / l1) - lam * (o2 * (1.0 / l2))
        o = o * lax.rsqrt(jnp.mean(o * o, axis=0, keepdims=True) + EPS)
        o = o * subln * (1.0 - lam_init)
        return o.T.astype(BF16)

    def one_shot(ks, vt, qm):
        s = _nt_dot(ks, qm)
        m = jnp.max(s, axis=0, keepdims=True)
        p = jnp.exp2(s - m)
        l = jnp.sum(p, axis=0, keepdims=True)
        return jnp.dot(vt, p.astype(BF16), preferred_element_type=F32), m, l

    tq, kc = ATTN_TQ, ATTN_KC
    n_kv = n_lat // kc

    def q_tile(i, carry):
        q0 = pl.multiple_of(i * tq, tq)
        qs = q_ref[pl.ds(q0, tq), :]
        qm = (qs * m_lo, qs * m_hi)

        def score(k0, size, dst_ref):
            ks = k_ref[pl.ds(k0, size), :]
            maxes = []
            for mp in range(2):
                s = _nt_dot(ks, qm[mp])
                dst_ref[mp] = s
                maxes.append(jnp.max(s, axis=0, keepdims=True))
            return tuple(maxes)

        def update(src_ref, smaxes, k0, size, c):
            vt = vt_ref[:, pl.ds(k0, size)]
            out = []
            for mp, acc_ref in ((0, acc1_ref), (1, acc2_ref)):
                s = src_ref[mp]
                smax = smaxes[mp]
                if c is None:
                    m_new = smax
                    p = jnp.exp2(s - m_new)
                    l_new = jnp.sum(p, axis=0, keepdims=True)
                    acc_ref[...] = jnp.dot(vt, p.astype(BF16), preferred_element_type=F32)
                else:
                    m, l = c[2 * mp], c[2 * mp + 1]
                    m_new = jnp.maximum(m, smax)
                    alpha = jnp.exp2(m - m_new)
                    p = jnp.exp2(s - m_new)
                    l_new = alpha * l + jnp.sum(p, axis=0, keepdims=True)
                    acc_ref[...] = alpha * acc_ref[...] + jnp.dot(vt, p.astype(BF16),
                                                                  preferred_element_type=F32)
                out += [m_new, l_new]
            return tuple(out)

        mx_c = score(n_lat, n_ctx, sc_ref)
        mx0 = score(0, kc, s0_ref)
        c = update(sc_ref, mx_c, n_lat, n_ctx, None)

        def pair(t, st):
            c, mx0 = st[:4], st[4:]
            ka = pl.multiple_of(2 * t * kc, kc)
            kb = pl.multiple_of(ka + kc, kc)
            mx1 = score(kb, kc, s1_ref)
            c = update(s0_ref, mx0, ka, kc, c)
            mx0 = score(pl.multiple_of(kb + kc, kc), kc, s0_ref)
            return update(s1_ref, mx1, kb, kc, c) + mx0

        st = lax.fori_loop(0, n_kv // 2 - 1, pair, c + mx0)
        c, mx0 = st[:4], st[4:]
        mx1 = score((n_kv - 1) * kc, kc, s1_ref)
        c = update(s0_ref, mx0, (n_kv - 2) * kc, kc, c)
        m1, l1, m2, l2 = update(s1_ref, mx1, (n_kv - 1) * kc, kc, c)
        o_ref[pl.ds(q0, tq), :] = finish(acc1_ref[...], l1, acc2_ref[...], l2)
        return carry

    lax.fori_loop(0, n_lat // tq, q_tile, 0)

    if do_ctx:
        qs = q_ref[n_lat:s_tot, :]
        kctx = k_ref[n_lat:s_tot, :]
        vctx = vt_ref[:, n_lat:s_tot]
        o1, _, l1 = one_shot(kctx, vctx, qs * m_lo)
        o2, _, l2 = one_shot(kctx, vctx, qs * m_hi)
        o_ref[n_lat:s_tot, :] = finish(o1, l1, o2, l2)
    else:
        o_ref[n_lat:s_tot, :] = jnp.zeros((n_ctx, LANES), BF16)


def _attn_a(proj, lam_qk, subln_col, n_lat, do_ctx, lam_init):
    b, s_tot, _ = proj.shape
    n_ctx = s_tot - n_lat
    kern = functools.partial(_attn_a_kernel, n_lat=n_lat, n_ctx=n_ctx, do_ctx=do_ctx, lam_init=lam_init)
    blk = lambda off: pl.BlockSpec((None, s_tot, LANES), lambda bi, h: (bi, 0, off + h))
    return pl.pallas_call(
        kern,
        grid=(b, HA),
        in_specs=[
            blk(0), blk(4), blk(8),
            pl.BlockSpec((4, HEAD_DIM), lambda bi, h: (0, 0)),
            pl.BlockSpec((2 * HEAD_DIM, 1), lambda bi, h: (0, 0)),
        ],
        out_specs=pl.BlockSpec((None, s_tot, LANES), lambda bi, h: (bi, 0, h)),
        out_shape=jax.ShapeDtypeStruct((b, s_tot, HA * LANES), BF16),
        scratch_shapes=[
            pltpu.VMEM((LANES, s_tot), BF16),
            pltpu.VMEM((LANES, ATTN_TQ), F32),
            pltpu.VMEM((LANES, ATTN_TQ), F32),
            pltpu.VMEM((2, ATTN_KC, ATTN_TQ), F32),
            pltpu.VMEM((2, ATTN_KC, ATTN_TQ), F32),
            pltpu.VMEM((2, n_ctx, ATTN_TQ), F32),
        ],
        name="attn_diff",
        compiler_params=_cparams(("parallel", "parallel")),
    )(proj, proj, proj, lam_qk, subln_col)


def _attn_b_kernel(sink_ref, q_ref, k_ref, v_ref, o_ref, vt_ref, sc0_ref, sw0_ref, sc1_ref, sw1_ref,
                   *, n_lat, n_ctx, do_ctx):
    s_tot = n_lat + n_ctx
    j = pl.program_id(1)
    _transpose_into(vt_ref, v_ref, s_tot)

    lane = lax.broadcasted_iota(jnp.int32, (1, LANES), 1)
    m_lo = (lane < HEAD_DIM).astype(BF16)
    m_hi = (lane >= HEAD_DIM).astype(BF16)
    nq = BLOCK
    win = 3 * BLOCK
    sink_row = jnp.concatenate(
        [jnp.full((1, nq), sink_ref[j * GB + g] * LOG2E, F32) for g in range(GB)], axis=1)
    rc = (lax.broadcasted_iota(jnp.int32, (win, nq), 0) - lax.broadcasted_iota(jnp.int32, (win, nq), 1))
    v0 = pl.multiple_of(j * HEAD_DIM, HEAD_DIM)

    def q_stack(q0):
        qa = q_ref[pl.ds(q0, nq), 0:LANES]
        qb = q_ref[pl.ds(q0, nq), LANES:2 * LANES]
        return jnp.concatenate([qa * m_lo, qa * m_hi, qb * m_lo, qb * m_hi], axis=0)

    def emit(q0, o):
        for pr in range(2):
            pair = jnp.concatenate([o[:, (2 * pr) * nq:(2 * pr + 1) * nq],
                                    o[:, (2 * pr + 1) * nq:(2 * pr + 2) * nq]], axis=0)
            o_ref[pl.ds(q0, nq), pr * LANES:(pr + 1) * LANES] = pair.T.astype(BF16)

    kctx = k_ref[n_lat:s_tot, :]
    vctx = vt_ref[pl.ds(v0, HEAD_DIM), n_lat:s_tot]

    def win_start(i):
        return pl.multiple_of(jnp.clip((i - 1) * nq, 0, n_lat - win), nq)

    def score(i, sc_ref, sw_ref):
        qs = q_stack(pl.multiple_of(i * nq, nq))
        start = win_start(i)
        sc_ref[...] = _nt_dot(kctx, qs)
        s_w = _nt_dot(k_ref[pl.ds(start, win), :], qs)
        rel = rc + (start - i * nq)
        ok = (rel <= WINDOW) & (rel >= -WINDOW)
        sw_ref[...] = jnp.where(jnp.concatenate([ok] * GB, axis=1), s_w, NEG)

    def attend(i, sc_ref, sw_ref):
        vw = vt_ref[pl.ds(v0, HEAD_DIM), pl.ds(win_start(i), win)]
        s_c = sc_ref[...]
        s_w = sw_ref[...]
        m = jnp.maximum(jnp.maximum(jnp.max(s_c, axis=0, keepdims=True),
                                    jnp.max(s_w, axis=0, keepdims=True)), sink_row)
        p_c = jnp.exp2(s_c - m)
        p_w = jnp.exp2(s_w - m)
        l = (jnp.sum(p_c, axis=0, keepdims=True) + jnp.sum(p_w, axis=0, keepdims=True)
             + jnp.exp2(sink_row - m))
        o = (jnp.dot(vctx, p_c.astype(BF16), preferred_element_type=F32)
             + jnp.dot(vw, p_w.astype(BF16), preferred_element_type=F32))
        emit(pl.multiple_of(i * nq, nq), o * (1.0 / l))

    n_blk = n_lat // nq
    score(0, sc0_ref, sw0_ref)

    def pair(t, carry):
        score(2 * t + 1, sc1_ref, sw1_ref)
        attend(2 * t, sc0_ref, sw0_ref)
        score(2 * t + 2, sc0_ref, sw0_ref)
        attend(2 * t + 1, sc1_ref, sw1_ref)
        return carry

    lax.fori_loop(0, n_blk // 2 - 1, pair, 0)
    score(n_blk - 1, sc1_ref, sw1_ref)
    attend(n_blk - 2, sc0_ref, sw0_ref)
    attend(n_blk - 1, sc1_ref, sw1_ref)

    for cb in range(n_ctx // nq):
        q0 = n_lat + cb * nq
        if do_ctx:
            qs = q_stack(q0)
            s_c = _nt_dot(kctx, qs)
            m = jnp.maximum(jnp.max(s_c, axis=0, keepdims=True), sink_row)
            p_c = jnp.exp2(s_c - m)
            l = jnp.sum(p_c, axis=0, keepdims=True) + jnp.exp2(sink_row - m)
            o = jnp.dot(vctx, p_c.astype(BF16), preferred_element_type=F32)
            emit(q0, o * (1.0 / l))
        else:
            o_ref[q0:q0 + nq, :] = jnp.zeros((nq, 2 * LANES), BF16)


def _attn_b(proj, sink, n_lat, do_ctx):
    b, s_tot, _ = proj.shape
    n_ctx = s_tot - n_lat
    kern = functools.partial(_attn_b_kernel, n_lat=n_lat, n_ctx=n_ctx, do_ctx=do_ctx)
    return pl.pallas_call(
        kern,
        grid_spec=pltpu.PrefetchScalarGridSpec(
            num_scalar_prefetch=1,
            grid=(b, KVB),
            in_specs=[
                pl.BlockSpec((None, s_tot, 2 * LANES), lambda bi, j, s: (bi, 0, 6 + j)),
                pl.BlockSpec((None, s_tot, LANES), lambda bi, j, s: (bi, 0, 16 + j)),
                pl.BlockSpec((None, s_tot, LANES), lambda bi, j, s: (bi, 0, 18)),
            ],
            out_specs=pl.BlockSpec((None, s_tot, 2 * LANES), lambda bi, j, s: (bi, 0, j)),
            scratch_shapes=[
                pltpu.VMEM((LANES, s_tot), BF16),
                pltpu.VMEM((n_ctx, GB * BLOCK), F32),
                pltpu.VMEM((3 * BLOCK, GB * BLOCK), F32),
                pltpu.VMEM((n_ctx, GB * BLOCK), F32),
                pltpu.VMEM((3 * BLOCK, GB * BLOCK), F32),
            ],
        ),
        out_shape=jax.ShapeDtypeStruct((b, s_tot, HB * HEAD_DIM), BF16),
        name="attn_win",
        compiler_params=_cparams(("parallel", "parallel")),
    )(sink, proj, proj, proj)


def _route_t(logits_t, bias_col):
    s = 1.0 / (1.0 + jnp.exp(-logits_t))
    sb = s + bias_col
    s_rows = [s[e:e + 1, :] for e in range(N_EXPERTS)]
    sb_rows = [sb[e:e + 1, :] for e in range(N_EXPERTS)]
    best = None
    gsel = None
    for g in range(N_GROUPS):
        a, b, c, d = sb_rows[4 * g:4 * g + 4]
        hi1, lo1 = jnp.maximum(a, b), jnp.minimum(a, b)
        hi2, lo2 = jnp.maximum(c, d), jnp.minimum(c, d)
        gs = jnp.maximum(hi1, hi2) + jnp.maximum(jnp.minimum(hi1, hi2), jnp.maximum(lo1, lo2))
        if g == 0:
            best, gsel = gs, jnp.zeros_like(gs, dtype=jnp.int32)
        else:
            better = gs > best
            gsel = jnp.where(better, g, gsel)
            best = jnp.where(better, gs, best)
    masked = [jnp.where(gsel == (e // EXPERTS_PER_GROUP), sb_rows[e], -jnp.inf) for e in range(N_EXPERTS)]

    def argtop(vals):
        v, i = vals[0], jnp.zeros_like(gsel)
        for e in range(1, N_EXPERTS):
            better = vals[e] > v
            i = jnp.where(better, e, i)
            v = jnp.where(better, vals[e], v)
        return i

    i1 = argtop(masked)
    i2 = argtop([jnp.where(i1 == e, -jnp.inf, masked[e]) for e in range(N_EXPERTS)])
    w1 = sum(jnp.where(i1 == e, s_rows[e], 0.0) for e in range(N_EXPERTS))
    w2 = sum(jnp.where(i2 == e, s_rows[e], 0.0) for e in range(N_EXPERTS))
    inv = 1.0 / (w1 + w2)
    w1, w2 = w1 * inv, w2 * inv
    swap = i2 < i1
    lo = jnp.where(swap, i2, i1) - gsel * EXPERTS_PER_GROUP
    hi = jnp.where(swap, i1, i2) - gsel * EXPERTS_PER_GROUP
    g_lo = jnp.where(swap, w2, w1)
    g_hi = jnp.where(swap, w1, w2)
    pair = jnp.where(lo == 0, hi - 1, jnp.where(lo == 1, hi + 1, PAIRS_PER_GROUP - 1))
    return gsel * PAIRS_PER_GROUP + pair, g_lo, g_hi


def _route_kernel(lat_ref, ctx_ref, oa_ref, ob_ref, wo_ref, mod_ref, modc_ref, g_ref, wr_ref, br_ref, tri_ref,
                  xo_ref, slab_ref, info_ref, cnt_ref, base_ref, *, ctx_t0):
    n_sub, tm = lat_ref.shape[0], lat_ref.shape[1]

    @pl.when((pl.program_id(0) == 0) & (pl.program_id(1) == 0))
    def _():
        base_ref[...] = jnp.zeros_like(base_ref)

    is_ctx = pl.program_id(1) >= ctx_t0
    row128 = lax.broadcasted_iota(jnp.int32, (LANES, tm), 0)
    rowb = lax.broadcasted_iota(jnp.int32, (BUCKET_ROWS, tm), 0)
    row8 = lax.broadcasted_iota(jnp.int32, (8, tm), 0)
    slab_ref[...] = jnp.zeros_like(slab_ref)
    base = base_ref[...]
    for j in range(n_sub):
        mod = jnp.where(is_ctx, modc_ref[...], mod_ref[j])
        mix = jnp.dot(oa_ref[j], wo_ref[0:D_QA, :], preferred_element_type=F32)
        mix += jnp.dot(ob_ref[j], wo_ref[D_QA:2 * D_QA, :], preferred_element_type=F32)
        x = jnp.where(is_ctx, ctx_ref[j], lat_ref[j]) + mod[2:3, :] * mix
        xo_ref[j] = x
        h = _rms(x) * g_ref[...] * (1.0 + mod[4:5, :]) + mod[3:4, :]
        h_hi, h_lo = _split_bf16(h)
        lt = _nt_dot(wr_ref[...], h_hi)
        lt2 = _nt_dot(wr_ref[0:N_EXPERTS, :], h_lo)
        logits_t = lt[0:N_EXPERTS] + lt[N_EXPERTS:2 * N_EXPERTS] + lt2
        bucket, g_lo, g_hi = _route_t(logits_t, br_ref[...])

        gate_t = jnp.where(row128 == 0, g_lo, jnp.where(row128 == 1, g_hi, 0.0))
        for s in range(OUT_SLAB):
            slab_ref[j, pl.ds(s, tm, stride=IN_SLAB), :] = h[:, s * LANES:(s + 1) * LANES]
        slab_ref[j, pl.ds(OUT_SLAB, tm, stride=IN_SLAB), :] = gate_t.T

        member = rowb == bucket
        ranks = jnp.dot(member.astype(BF16), tri_ref[...], preferred_element_type=F32) + base[:, 0:1]
        rank = jnp.sum(jnp.where(member, ranks, 0.0), axis=0, keepdims=True)
        info_ref[j] = jnp.where(row8 == 0, bucket.astype(F32), jnp.where(row8 == 1, rank, 0.0))
        base = base + jnp.sum(member.astype(F32), axis=1, keepdims=True)
    base_ref[...] = base
    cnt_ref[...] = base


def _route(lat_src, ctx_src, ctx_off, alias_lat, oa, ob, w_out, mod, gnorm, wr_t, br_col, n_lat, n_rows, s_tot):
    b = lat_src.shape[0]
    tm = ROUTE_TM
    g = ROUTE_SAMPLES
    n_t = n_rows // tm
    ctx_t0 = n_lat // tm
    tri = (jnp.arange(tm)[:, None] < jnp.arange(tm)[None, :]).astype(BF16)
    const = lambda shape: pl.BlockSpec(shape, lambda bi, t: (0,) * len(shape))
    tile = lambda width: pl.BlockSpec((g, tm, width), lambda bi, t: (bi, t, 0))
    xs, slabs, info, counts = pl.pallas_call(
        functools.partial(_route_kernel, ctx_t0=ctx_t0),
        grid=(b // g, n_t),
        in_specs=[
            *_token_specs(tm, ctx_t0, ctx_off, g),
            tile(D_QA), tile(D_QA),
            const((D_MODEL, D_MODEL)),
            pl.BlockSpec((g, 6, D_MODEL), lambda bi, t: (bi, 0, 0)),
            pl.BlockSpec((None, 6, D_MODEL), lambda bi, t: (b, 0, 0)),
            const((1, D_MODEL)),
            const((2 * N_EXPERTS, D_MODEL)),
            const((N_EXPERTS, 1)),
            const((tm, tm)),
        ],
        out_specs=[
            tile(D_MODEL),
            pl.BlockSpec((g, tm * IN_SLAB, LANES), lambda bi, t: (bi, t, 0)),
            pl.BlockSpec((g, 8, tm), lambda bi, t: (bi, 0, t)),
            const((BUCKET_ROWS, LANES)),
        ],
        out_shape=[
            jax.ShapeDtypeStruct((b, s_tot, D_MODEL), F32),
            jax.ShapeDtypeStruct((b, n_t * tm * IN_SLAB, LANES), F32),
            jax.ShapeDtypeStruct((b, 8, n_t * tm), F32),
            jax.ShapeDtypeStruct((BUCKET_ROWS, LANES), F32),
        ],
        scratch_shapes=[pltpu.VMEM((BUCKET_ROWS, LANES), F32)],
        input_output_aliases={0: 0} if alias_lat else {},
        name="mix_route",
        compiler_params=_cparams(("arbitrary", "arbitrary")),
    )(lat_src, ctx_src, oa, ob, w_out, mod, mod, gnorm, wr_t, br_col, tri)
    return xs, slabs.reshape(b * n_t * tm * IN_SLAB, LANES), info, counts


def _row_copy(src, dst, sem):
    return pltpu.make_async_copy(src, dst, sem)


def _slab(ref, i, n):
    return ref.at[pl.ds(pl.multiple_of(i * n, n), n), :]


def _slab_rows(ref, s, tm, n):
    return ref[pl.ds(s, tm, stride=n), :]


def _permute_kernel(ends_ref, pos_ref, slab_ref, xs_hbm, zero_ref, sem, zsem):
    tm = slab_ref.shape[0] // IN_SLAB
    tile = MOE_TMG * IN_SLAB

    @pl.when((pl.program_id(0) == 0) & (pl.program_id(1) == 0))
    def _():
        zero_ref[...] = jnp.zeros_like(zero_ref)

        def zero_tile(first_token):
            dst = xs_hbm.at[pl.ds(pl.multiple_of(first_token * IN_SLAB, tile), tile), :]
            cp = _row_copy(zero_ref, dst, zsem)
            cp.start()
            cp.wait()

        def pad_fill(g, c):
            prev = ends_ref[jnp.maximum(g - 1, 0)]
            size = ends_ref[g] - jnp.where(g == 0, 0, prev)

            @pl.when(size > 0)
            def _():
                zero_tile(ends_ref[g] - MOE_TMG)
            return c

        lax.fori_loop(0, N_BUCKETS, pad_fill, 0)

        n_tiles = xs_hbm.shape[0] // tile

        def tail_fill(t0, c):
            @pl.when(t0 * MOE_TMG >= ends_ref[N_BUCKETS - 1])
            def _():
                zero_tile(t0 * MOE_TMG)
            return c

        lax.fori_loop(n_tiles - N_BUCKETS, n_tiles, tail_fill, 0)

    def body(r, c):
        _row_copy(_slab(slab_ref, r, IN_SLAB), _slab(xs_hbm, pos_ref[0, 0, r], IN_SLAB), sem).start()
        return c

    lax.fori_loop(0, tm, body, 0, unroll=8)
    _row_copy(slab_ref, xs_hbm.at[pl.ds(0, tm * IN_SLAB), :], sem).wait()


def _permute(slabs, pos, ends, cap, tm):
    n_t = pos.shape[0]
    return pl.pallas_call(
        _permute_kernel,
        grid_spec=pltpu.PrefetchScalarGridSpec(
            num_scalar_prefetch=1,
            grid=(1, n_t),
            in_specs=[
                pl.BlockSpec((1, 1, tm), lambda bi, t, e: (t, 0, 0), memory_space=pltpu.SMEM),
                pl.BlockSpec((tm * IN_SLAB, LANES), lambda bi, t, e: (t, 0)),
            ],
            out_specs=pl.BlockSpec(memory_space=pl.ANY),
            scratch_shapes=[
                pltpu.VMEM((MOE_TMG * IN_SLAB, LANES), F32),
                pltpu.SemaphoreType.DMA(()),
                pltpu.SemaphoreType.DMA(()),
            ],
        ),
        out_shape=jax.ShapeDtypeStruct((cap * IN_SLAB, LANES), F32),
        name="moe_permute",
        compiler_params=_cparams(("arbitrary", "arbitrary")),
    )(ends, pos, slabs)


def _ffn_kernel(ea_ref, eb_ref, val_ref, x_ref, w1a_ref, w3a_ref, w2a_ref, w1b_ref, w3b_ref, w2b_ref, o_ref):
    i = pl.program_id(0)
    tm = x_ref.shape[0] // IN_SLAB

    @pl.when(val_ref[i] > 0)
    def _():
        x = jnp.concatenate([_slab_rows(x_ref, s, tm, IN_SLAB).astype(BF16) for s in range(OUT_SLAB)],
                            axis=1)
        gates = _slab_rows(x_ref, OUT_SLAB, tm, IN_SLAB)
        acc = None
        for k, (w1_ref, w3_ref, w2_ref) in enumerate(((w1a_ref, w3a_ref, w2a_ref), (w1b_ref, w3b_ref, w2b_ref))):
            a = jnp.dot(x, w1_ref[...], preferred_element_type=F32)
            b = jnp.dot(x, w3_ref[...], preferred_element_type=F32)
            u = (a * (1.0 / (1.0 + jnp.exp(-a)))) * b * gates[:, k:k + 1]
            y = jnp.dot(u.astype(BF16), w2_ref[...], preferred_element_type=F32)
            acc = y if acc is None else acc + y
        for s in range(OUT_SLAB):
            o_ref[pl.ds(s, tm, stride=OUT_SLAB), :] = acc[:, s * LANES:(s + 1) * LANES]

    @pl.when(val_ref[i] == 0)
    def _():
        o_ref[...] = jnp.zeros_like(o_ref)


def _ffn(xs_sorted, tile_ea, tile_eb, tile_valid, w1, w3, w2, layer):
    n_tiles = xs_sorted.shape[0] // (MOE_TMG * IN_SLAB)
    up = lambda sel: pl.BlockSpec((None, None, D_MODEL, D_EXPERT),
                                  lambda i, ea, eb, v: (layer, sel(ea, eb)[i], 0, 0))
    down = lambda sel: pl.BlockSpec((None, None, D_EXPERT, D_MODEL),
                                    lambda i, ea, eb, v: (layer, sel(ea, eb)[i], 0, 0))
    first = lambda ea, eb: ea
    second = lambda ea, eb: eb
    return pl.pallas_call(
        _ffn_kernel,
        grid_spec=pltpu.PrefetchScalarGridSpec(
            num_scalar_prefetch=3,
            grid=(n_tiles,),
            in_specs=[
                pl.BlockSpec((MOE_TMG * IN_SLAB, LANES), lambda i, ea, eb, v: (i, 0)),
                up(first), up(first), down(first),
                up(second), up(second), down(second),
            ],
            out_specs=pl.BlockSpec((MOE_TMG * OUT_SLAB, LANES), lambda i, ea, eb, v: (i, 0)),
        ),
        out_shape=jax.ShapeDtypeStruct((n_tiles * MOE_TMG * OUT_SLAB, LANES), F32),
        name="moe_ffn",
        compiler_params=_cparams(("arbitrary",)),
    )(tile_ea, tile_eb, tile_valid, xs_sorted, w1, w3, w2, w1, w3, w2)


def _combine_kernel(pos_ref, x_ref, mod_ref, modc_ref, gfin_ref, ys_hbm, o_ref, buf_ref, sem, *, n_lat, final):
    tm = x_ref.shape[0]

    def body(r, c):
        _row_copy(_slab(ys_hbm, pos_ref[0, 0, r], OUT_SLAB), _slab(buf_ref, r, OUT_SLAB), sem).start()
        return c

    lax.fori_loop(0, tm, body, 0, unroll=8)
    _row_copy(ys_hbm.at[pl.ds(0, tm * OUT_SLAB), :], buf_ref, sem).wait()
    y = jnp.concatenate([_slab_rows(buf_ref, s, tm, OUT_SLAB) for s in range(OUT_SLAB)], axis=1)
    row = pl.program_id(1) * tm + lax.broadcasted_iota(jnp.int32, (tm, 1), 0)
    gate = jnp.where(row >= n_lat, modc_ref[5:6, :], mod_ref[5:6, :])
    out = x_ref[...] + gate * y
    if final:
        out = _rms(out) * gfin_ref[...]
    o_ref[...] = out


def _combine(xs, ys, pos, mod, gfin, n_lat, n_rows, tm, final):
    b, s_tot, _ = xs.shape
    n_t = n_rows // tm
    x_spec = pl.BlockSpec((None, tm, D_MODEL), lambda bi, t: (bi, t, 0))
    return pl.pallas_call(
        functools.partial(_combine_kernel, n_lat=n_lat, final=final),
        grid=(b, n_t),
        in_specs=[
            pl.BlockSpec((1, 1, tm), lambda bi, t: (bi * n_t + t, 0, 0), memory_space=pltpu.SMEM),
            x_spec,
            pl.BlockSpec((None, 6, D_MODEL), lambda bi, t: (bi, 0, 0)),
            pl.BlockSpec((None, 6, D_MODEL), lambda bi, t: (b, 0, 0)),
            pl.BlockSpec((1, D_MODEL), lambda bi, t: (0, 0)),
            pl.BlockSpec(memory_space=pl.ANY),
        ],
        out_specs=x_spec,
        out_shape=jax.ShapeDtypeStruct((b, n_rows, D_MODEL) if final else xs.shape, F32),
        scratch_shapes=[pltpu.VMEM((tm * OUT_SLAB, LANES), F32), pltpu.SemaphoreType.DMA(())],
        input_output_aliases={} if final else {1: 0},
        name="moe_combine",
        compiler_params=_cparams(("arbitrary", "arbitrary")),
    )(pos, xs, mod, mod, gfin, ys)


def _moe(xs, rows, info, counts, mod, w1, w3, w2, gfin, *, layer, n_lat, n_rows, final):
    b = xs.shape[0]
    n_tok = b * n_rows

    tmg = MOE_TMG
    cap = (n_tok // tmg + N_BUCKETS) * tmg
    cnt = counts[:N_BUCKETS, 0].astype(jnp.int32)
    padded = (cnt + tmg - 1) // tmg * tmg
    ends = jnp.cumsum(padded)
    starts = ends - padded
    bucket = info[:, 0, :].astype(jnp.int32)
    start_of = sum(jnp.where(bucket == k, starts[k], 0) for k in range(N_BUCKETS))
    pos = start_of + info[:, 1, :].astype(jnp.int32)
    tm_p = next(t for t in PERMUTE_TMS if n_tok % t == 0)
    tm_c = n_rows // COMBINE_TILES
    tile_start = jnp.arange(cap // tmg, dtype=jnp.int32) * tmg
    tile_valid = (tile_start < ends[-1]).astype(jnp.int32)
    tile_bkt = sum((tile_start >= ends[k]).astype(jnp.int32) for k in range(N_BUCKETS - 1))
    last_bkt = jnp.max(jnp.where(tile_valid > 0, tile_bkt, 0))
    tile_bkt = jnp.where(tile_valid > 0, tile_bkt, last_bkt)
    pair_lo = (0, 0, 0, 1, 1, 2)
    pair_hi = (1, 2, 3, 2, 3, 3)
    grp, pair = tile_bkt // PAIRS_PER_GROUP, tile_bkt % PAIRS_PER_GROUP
    tile_ea = grp * EXPERTS_PER_GROUP + sum(jnp.where(pair == p, pair_lo[p], 0) for p in range(PAIRS_PER_GROUP))
    tile_eb = grp * EXPERTS_PER_GROUP + sum(jnp.where(pair == p, pair_hi[p], 0) for p in range(PAIRS_PER_GROUP))

    xs_sorted = _permute(rows, pos.reshape(n_tok // tm_p, 1, tm_p), ends.astype(jnp.int32), cap, tm_p)
    ys = _ffn(xs_sorted, tile_ea.astype(jnp.int32), tile_eb.astype(jnp.int32), tile_valid, w1, w3, w2, layer)
    return _combine(xs, ys, pos.reshape(n_tok // tm_c, 1, tm_c), mod, gfin, n_lat, n_rows, tm_c, final)


def _rope_tables(n_lat, n_ctx):
    pos = jnp.arange(n_lat)
    row = (pos // GRID_W).astype(F32)
    col = (pos % GRID_W).astype(F32)
    n_freq = HEAD_DIM // 4
    inv = 1.0 / (ROPE_BASE ** (jnp.arange(n_freq, dtype=F32) / n_freq))
    lane = jnp.arange(LANES) % HEAD_DIM
    axis = lane // 32
    second = (lane % 32) // 16
    freq = inv[lane % 16]
    ang = jnp.where(axis[None, :] == 0, row[:, None], col[:, None]) * freq[None, :]
    cos = jnp.cos(ang)
    sin = jnp.sin(ang) * jnp.where(second == 0, -1.0, 1.0)[None, :]
    cos = jnp.concatenate([cos, jnp.ones((n_ctx, LANES), F32)], axis=0)
    sin = jnp.concatenate([sin, jnp.zeros((n_ctx, LANES), F32)], axis=0)
    q_scale = HEAD_DIM ** -0.5 * LOG2E
    return cos * q_scale, sin * q_scale, cos, sin


def _proj_cols():
    base = jnp.arange(D_IN)
    kb0 = base[2048:2112]
    kb1 = base[2112:2176]
    return jnp.concatenate([base[:2048], kb0, kb0, kb1, kb1, base[2176:]])


def kernel(x, c, ctx, c_ctx, w_ada, b_ada, norm_attn, norm_ffn, w_in, w_out, lambda_qk, subln, sink, w_router,
           b_router, w1, w3, w2, norm_final):
    b, n_lat, _ = x.shape
    n_ctx = ctx.shape[1]
    depth = w_in.shape[0]

    s_tot = n_lat + n_ctx
    c_all = jnp.zeros((MOD_ROWS, D_MODEL), F32).at[:b].set(c).at[b].set(c_ctx)
    mod_all = _ada(c_all, w_ada, b_ada).reshape(depth, MOD_ROWS, 6, D_MODEL)
    tables = _rope_tables(n_lat, n_ctx)
    cols = _proj_cols()

    wr_hi, wr_lo = _split_bf16(w_router.T)
    wr_t = jnp.concatenate([wr_hi, wr_lo], axis=0)
    br_col = b_router.reshape(N_EXPERTS, 1)
    gfin = norm_final.reshape(1, D_MODEL)

    w1b, w3b, w2b = w1.astype(BF16), w3.astype(BF16), w2.astype(BF16)
    ctx_t0 = n_lat // PROJ_TM

    lat_src, ctx_src, ctx_off = x, ctx, 0
    out = None
    for l in range(depth):
        last = l == depth - 1
        lam_init = 0.8 - 0.6 * math.exp(-0.3 * l)
        mod = mod_all[l]
        w_in_l = w_in[l][:, cols].astype(BF16)
        proj = _proj(lat_src, ctx_src, ctx_off, mod, norm_attn[l].reshape(1, D_MODEL), w_in_l, tables, n_lat, s_tot)
        oa = _attn_a(proj, lambda_qk[l], subln[l].reshape(2 * HEAD_DIM, 1), n_lat, not last, lam_init)
        ob = _attn_b(proj, sink[l], n_lat, not last)
        n_rows = n_lat if last else s_tot
        in_place = l > 0
        xs, rows, info, counts = _route(lat_src, ctx if in_place else ctx_src, 0 if in_place else ctx_off, in_place,
                                        oa, ob, w_out[l].astype(BF16), mod, norm_ffn[l].reshape(1, D_MODEL),
                                        wr_t, br_col, n_lat, n_rows, s_tot)
        res = _moe(xs, rows, info, counts, mod, w1b, w3b, w2b, gfin, layer=l, n_lat=n_lat, n_rows=n_rows,
                   final=last)
        if last:
            out = res
        else:
            lat_src, ctx_src, ctx_off = res, res, ctx_t0
    return out
```

```python
import functools
import math

import jax
import jax.numpy as jnp
from jax import lax
from jax.experimental import pallas as pl
from jax.experimental.pallas import tpu as pltpu

F32 = jnp.float32
BF16 = jnp.bfloat16

D_MODEL = 1024
HEAD_DIM = 64
HA = 4
HB = 8
KVB = 2
GB = HB // KVB
GRID_W = 64
WINDOW = 128
BLOCK = 128
ROPE_BASE = 10000.0
N_EXPERTS = 16
N_GROUPS = 4
EXPERTS_PER_GROUP = N_EXPERTS // N_GROUPS
D_EXPERT = 512
EPS = 1e-6
NEG = -1e30
LOG2E = 1.4426950408889634

LANES = 128
D_QA = HA * 2 * HEAD_DIM
D_IN = 2304
N_PROJ_TILES = 19
D_PROJ = N_PROJ_TILES * LANES
Q_TILES = (0, 1, 2, 3, 12, 13, 14, 15)
K_TILES = (4, 5, 6, 7, 16, 17)

VMEM_LIMIT = 56 * 1024 * 1024

PROJ_TM = 256
ATTN_TQ = 1024
ATTN_KC = 512
ROUTE_TM = 256
ROUTE_SAMPLES = 2
PAIRS_PER_GROUP = 6
N_BUCKETS = N_GROUPS * PAIRS_PER_GROUP
BUCKET_ROWS = 32
MOE_TMG = 256
PERMUTE_TMS = (1024, 512, 256)
COMBINE_TILES = 4
OUT_SLAB = D_MODEL // LANES
IN_SLAB = 2 * OUT_SLAB
MOD_ROWS = 16


def _cparams(sem):
    return pltpu.CompilerParams(dimension_semantics=sem, vmem_limit_bytes=VMEM_LIMIT)


def _nt_dot(a, b):
    return lax.dot_general(a, b, (((1,), (1,)), ((), ())), preferred_element_type=F32)


def _split_bf16(x):
    hi = x.astype(BF16)
    lo = (x - hi.astype(F32)).astype(BF16)
    return hi, lo


def _rms(x):
    return x * lax.rsqrt(jnp.mean(x * x, axis=-1, keepdims=True) + EPS)


def _ada_kernel(c_ref, w_ref, b_ref, o_ref):
    c = c_ref[...]
    a = c * (1.0 / (1.0 + jnp.exp(-c)))
    a_hi, a_lo = _split_bf16(a)
    w_hi, w_lo = _split_bf16(w_ref[...])
    acc = jnp.dot(a_hi, w_hi, preferred_element_type=F32)
    acc += jnp.dot(a_hi, w_lo, preferred_element_type=F32)
    acc += jnp.dot(a_lo, w_hi, preferred_element_type=F32)
    o_ref[...] = acc + b_ref[...]


def _ada(c_all, w_ada, b_ada):
    depth = w_ada.shape[0]
    n_out = w_ada.shape[2]
    tn = D_MODEL
    return pl.pallas_call(
        _ada_kernel,
        grid=(depth, n_out // tn),
        in_specs=[
            pl.BlockSpec((MOD_ROWS, D_MODEL), lambda l, n: (0, 0)),
            pl.BlockSpec((None, D_MODEL, tn), lambda l, n: (l, 0, n)),
            pl.BlockSpec((None, 1, tn), lambda l, n: (l, 0, n)),
        ],
        out_specs=pl.BlockSpec((None, MOD_ROWS, tn), lambda l, n: (l, 0, n)),
        out_shape=jax.ShapeDtypeStruct((depth, MOD_ROWS, n_out), F32),
        name="ada_mod",
        compiler_params=_cparams(("parallel", "parallel")),
    )(c_all, w_ada, b_ada.reshape(depth, 1, n_out))


def _token_specs(tm, ctx_t0, ctx_off, samples=None):
    lat = pl.BlockSpec((samples, tm, D_MODEL), lambda bi, t, *_: (bi, jnp.minimum(t, ctx_t0 - 1), 0))
    ctx = pl.BlockSpec((samples, tm, D_MODEL), lambda bi, t, *_: (bi, ctx_off + jnp.maximum(t - ctx_t0, 0), 0))
    return lat, ctx


def _proj_kernel(lat_ref, ctx_ref, mod_ref, g_ref, w_ref, cq_ref, sq_ref, ck_ref, sk_ref, o_ref, *, ctx_t0):
    x = jnp.where(pl.program_id(1) >= ctx_t0, ctx_ref[...], lat_ref[...])
    h = _rms(x) * g_ref[...] * (1.0 + mod_ref[1:2, :]) + mod_ref[0:1, :]
    acc = jnp.dot(h.astype(BF16), w_ref[...], preferred_element_type=F32)
    tm = x.shape[0]
    lane = lax.broadcasted_iota(jnp.int32, (tm, LANES), 1)
    first_half = (lane % 32) < 16
    for t in range(N_PROJ_TILES):
        a = acc[:, t * LANES:(t + 1) * LANES]
        if t in Q_TILES or t in K_TILES:
            cos, sin = (cq_ref, sq_ref) if t in Q_TILES else (ck_ref, sk_ref)
            partner = jnp.where(first_half, pltpu.roll(a, LANES - 16, 1), pltpu.roll(a, 16, 1))
            a = a * cos[...] + partner * sin[...]
        o_ref[:, t * LANES:(t + 1) * LANES] = a.astype(BF16)


def _proj(lat_src, ctx_src, ctx_off, mod, gnorm, w, tables, n_lat, s_tot):
    b = lat_src.shape[0]
    tm = PROJ_TM
    n_t = s_tot // tm
    ctx_t0 = n_lat // tm
    tab_spec = pl.BlockSpec((tm, LANES), lambda bi, t: (t, 0))
    return pl.pallas_call(
        functools.partial(_proj_kernel, ctx_t0=ctx_t0),
        grid=(b, n_t),
        in_specs=[
            *_token_specs(tm, ctx_t0, ctx_off),
            pl.BlockSpec((None, 6, D_MODEL), lambda bi, t: (jnp.where(t >= ctx_t0, b, bi), 0, 0)),
            pl.BlockSpec((1, D_MODEL), lambda bi, t: (0, 0)),
            pl.BlockSpec((D_MODEL, D_PROJ), lambda bi, t: (0, 0)),
            tab_spec, tab_spec, tab_spec, tab_spec,
        ],
        out_specs=pl.BlockSpec((None, tm, D_PROJ), lambda bi, t: (bi, t, 0)),
        out_shape=jax.ShapeDtypeStruct((b, s_tot, D_PROJ), BF16),
        name="in_proj",
        compiler_params=_cparams(("parallel", "parallel")),
    )(lat_src, ctx_src, mod, gnorm, w, *tables)


def _transpose_into(vt_ref, v_ref, s_tot):
    for c in range(s_tot // LANES):
        blk = v_ref[c * LANES:(c + 1) * LANES, :].astype(F32)
        vt_ref[:, c * LANES:(c + 1) * LANES] = blk.T.astype(BF16)


def _attn_a_kernel(q_ref, k_ref, v_ref, lam_ref, subln_ref, o_ref, vt_ref, acc1_ref, acc2_ref,
                   s0_ref, s1_ref, *, n_lat, n_ctx, do_ctx, lam_init):
    s_tot = n_lat + n_ctx
    _transpose_into(vt_ref, v_ref, s_tot)

    lq = lam_ref[...]
    lam = (jnp.exp(jnp.sum(lq[0:1] * lq[1:2], axis=1, keepdims=True))
           - jnp.exp(jnp.sum(lq[2:3] * lq[3:4], axis=1, keepdims=True)) + lam_init)
    lane = lax.broadcasted_iota(jnp.int32, (1, LANES), 1)
    m_lo = (lane < HEAD_DIM).astype(BF16)
    m_hi = (lane >= HEAD_DIM).astype(BF16)
    subln = subln_ref[...]

    def finish(o1, l1, o2, l2):
        o = o1 * (1.0 / l1) - lam * (o2 * (1.0 / l2))
        o = o * lax.rsqrt(jnp.mean(o * o, axis=0, keepdims=True) + EPS)
        o = o * subln * (1.0 - lam_init)
        return o.T.astype(BF16)

    def one_shot(ks, vt, qm):
        s = _nt_dot(ks, qm)
        m = jnp.max(s, axis=0, keepdims=True)
        p = jnp.exp2(s - m)
        l = jnp.sum(p, axis=0, keepdims=True)
        return jnp.dot(vt, p.astype(BF16), preferred_element_type=F32), m, l

    tq, kc = ATTN_TQ, ATTN_KC
    n_kv = n_lat // kc

    def q_tile(i, carry):
        q0 = pl.multiple_of(i * tq, tq)
        qs = q_ref[pl.ds(q0, tq), :]
        qm = (qs * m_lo, qs * m_hi)

        def score(k0, size, dst_ref):
            ks = k_ref[pl.ds(k0, size), :]
            maxes = []
            for mp in range(2):
                s = _nt_dot(ks, qm[mp])
                dst_ref[mp, 0:size, :] = s
                maxes.append(jnp.max(s, axis=0, keepdims=True))
            return tuple(maxes)

        def update(src_ref, smaxes, k0, size, c):
            vt = vt_ref[:, pl.ds(k0, size)]
            out = []
            for mp, acc_ref in ((0, acc1_ref), (1, acc2_ref)):
                s = src_ref[mp, 0:size, :]
                smax = smaxes[mp]
                if c is None:
                    m_new = smax
                    p = jnp.exp2(s - m_new)
                    l_new = jnp.sum(p, axis=0, keepdims=True)
                    acc_ref[...] = jnp.dot(vt, p.astype(BF16), preferred_element_type=F32)
                else:
                    m, l = c[2 * mp], c[2 * mp + 1]
                    m_new = jnp.maximum(m, smax)
                    alpha = jnp.exp2(m - m_new)
                    p = jnp.exp2(s - m_new)
                    l_new = alpha * l + jnp.sum(p, axis=0, keepdims=True)
                    acc_ref[...] = alpha * acc_ref[...] + jnp.dot(vt, p.astype(BF16),
                                                                  preferred_element_type=F32)
                out += [m_new, l_new]
            return tuple(out)

        mx0 = score(0, kc, s0_ref)
        mx1 = score(kc, kc, s1_ref)
        c = update(s0_ref, mx0, 0, kc, None)

        def pair(t, st):
            c, mx1 = st[:4], st[4:]
            ka = pl.multiple_of((2 * t + 1) * kc, kc)
            kb = pl.multiple_of(ka + kc, kc)
            mx0 = score(kb, kc, s0_ref)
            c = update(s1_ref, mx1, ka, kc, c)
            mx1 = score(pl.multiple_of(kb + kc, kc), kc, s1_ref)
            return update(s0_ref, mx0, kb, kc, c) + mx1

        st = lax.fori_loop(0, n_kv // 2 - 2, pair, c + mx1)
        c, mx1 = st[:4], st[4:]
        last = (n_kv - 1) * kc
        mx0 = score(last - kc, kc, s0_ref)
        c = update(s1_ref, mx1, last - 2 * kc, kc, c)
        mx1 = score(last, kc + n_ctx, s1_ref)
        c = update(s0_ref, mx0, last - kc, kc, c)
        m1, l1, m2, l2 = update(s1_ref, mx1, last, kc + n_ctx, c)
        o_ref[pl.ds(q0, tq), :] = finish(acc1_ref[...], l1, acc2_ref[...], l2)
        return carry

    lax.fori_loop(0, n_lat // tq, q_tile, 0)

    if do_ctx:
        qs = q_ref[n_lat:s_tot, :]
        kctx = k_ref[n_lat:s_tot, :]
        vctx = vt_ref[:, n_lat:s_tot]
        o1, _, l1 = one_shot(kctx, vctx, qs * m_lo)
        o2, _, l2 = one_shot(kctx, vctx, qs * m_hi)
        o_ref[n_lat:s_tot, :] = finish(o1, l1, o2, l2)
    else:
        o_ref[n_lat:s_tot, :] = jnp.zeros((n_ctx, LANES), BF16)


def _attn_a(proj, lam_qk, subln_col, n_lat, do_ctx, lam_init):
    b, s_tot, _ = proj.shape
    n_ctx = s_tot - n_lat
    n_kv = n_lat // ATTN_KC
    assert n_lat % ATTN_TQ == 0 and n_lat % ATTN_KC == 0 and n_kv >= 4 and n_kv % 2 == 0, n_lat
    kern = functools.partial(_attn_a_kernel, n_lat=n_lat, n_ctx=n_ctx, do_ctx=do_ctx, lam_init=lam_init)
    blk = lambda off: pl.BlockSpec((None, s_tot, LANES), lambda bi, h: (bi, 0, off + h))
    return pl.pallas_call(
        kern,
        grid=(b, HA),
        in_specs=[
            blk(0), blk(4), blk(8),
            pl.BlockSpec((4, HEAD_DIM), lambda bi, h: (0, 0)),
            pl.BlockSpec((2 * HEAD_DIM, 1), lambda bi, h: (0, 0)),
        ],
        out_specs=pl.BlockSpec((None, s_tot, LANES), lambda bi, h: (bi, 0, h)),
        out_shape=jax.ShapeDtypeStruct((b, s_tot, HA * LANES), BF16),
        scratch_shapes=[
            pltpu.VMEM((LANES, s_tot), BF16),
            pltpu.VMEM((LANES, ATTN_TQ), F32),
            pltpu.VMEM((LANES, ATTN_TQ), F32),
            pltpu.VMEM((2, ATTN_KC, ATTN_TQ), F32),
            pltpu.VMEM((2, ATTN_KC + n_ctx, ATTN_TQ), F32),
        ],
        name="attn_diff",
        compiler_params=_cparams(("parallel", "parallel")),
    )(proj, proj, proj, lam_qk, subln_col)


def _attn_b_kernel(sink_ref, q_ref, k_ref, v_ref, o_ref, vt_ref, sc0_ref, sw0_ref, sc1_ref, sw1_ref,
                   *, n_lat, n_ctx, do_ctx):
    s_tot = n_lat + n_ctx
    j = pl.program_id(1)
    _transpose_into(vt_ref, v_ref, s_tot)

    lane = lax.broadcasted_iota(jnp.int32, (1, LANES), 1)
    m_lo = (lane < HEAD_DIM).astype(BF16)
    m_hi = (lane >= HEAD_DIM).astype(BF16)
    nq = BLOCK
    win = 3 * BLOCK
    sink_row = jnp.concatenate(
        [jnp.full((1, nq), sink_ref[j * GB + g] * LOG2E, F32) for g in range(GB)], axis=1)
    rc = (lax.broadcasted_iota(jnp.int32, (win, nq), 0) - lax.broadcasted_iota(jnp.int32, (win, nq), 1))
    v0 = pl.multiple_of(j * HEAD_DIM, HEAD_DIM)

    def q_stack(q0):
        qa = q_ref[pl.ds(q0, nq), 0:LANES]
        qb = q_ref[pl.ds(q0, nq), LANES:2 * LANES]
        return jnp.concatenate([qa * m_lo, qa * m_hi, qb * m_lo, qb * m_hi], axis=0)

    def emit(q0, o):
        for pr in range(2):
            pair = jnp.concatenate([o[:, (2 * pr) * nq:(2 * pr + 1) * nq],
                                    o[:, (2 * pr + 1) * nq:(2 * pr + 2) * nq]], axis=0)
            o_ref[pl.ds(q0, nq), pr * LANES:(pr + 1) * LANES] = pair.T.astype(BF16)

    kctx = k_ref[n_lat:s_tot, :]
    vctx = vt_ref[pl.ds(v0, HEAD_DIM), n_lat:s_tot]

    def win_start(i):
        return pl.multiple_of(jnp.clip((i - 1) * nq, 0, n_lat - win), nq)

    def score(i, sc_ref, sw_ref):
        qs = q_stack(pl.multiple_of(i * nq, nq))
        start = win_start(i)
        sc_ref[...] = _nt_dot(kctx, qs)
        s_w = _nt_dot(k_ref[pl.ds(start, win), :], qs)
        rel = rc + (start - i * nq)
        ok = (rel <= WINDOW) & (rel >= -WINDOW)
        sw_ref[...] = jnp.where(jnp.concatenate([ok] * GB, axis=1), s_w, NEG)

    def attend(i, sc_ref, sw_ref):
        vw = vt_ref[pl.ds(v0, HEAD_DIM), pl.ds(win_start(i), win)]
        s_c = sc_ref[...]
        s_w = sw_ref[...]
        m = jnp.maximum(jnp.maximum(jnp.max(s_c, axis=0, keepdims=True),
                                    jnp.max(s_w, axis=0, keepdims=True)), sink_row)
        p_c = jnp.exp2(s_c - m)
        p_w = jnp.exp2(s_w - m)
        l = (jnp.sum(p_c, axis=0, keepdims=True) + jnp.sum(p_w, axis=0, keepdims=True)
             + jnp.exp2(sink_row - m))
        o = (jnp.dot(vctx, p_c.astype(BF16), preferred_element_type=F32)
             + jnp.dot(vw, p_w.astype(BF16), preferred_element_type=F32))
        emit(pl.multiple_of(i * nq, nq), o * (1.0 / l))

    n_blk = n_lat // nq
    score(0, sc0_ref, sw0_ref)

    def pair(t, carry):
        score(2 * t + 1, sc1_ref, sw1_ref)
        attend(2 * t, sc0_ref, sw0_ref)
        score(2 * t + 2, sc0_ref, sw0_ref)
        attend(2 * t + 1, sc1_ref, sw1_ref)
        return carry

    lax.fori_loop(0, n_blk // 2 - 1, pair, 0)
    score(n_blk - 1, sc1_ref, sw1_ref)
    attend(n_blk - 2, sc0_ref, sw0_ref)
    attend(n_blk - 1, sc1_ref, sw1_ref)

    for cb in range(n_ctx // nq):
        q0 = n_lat + cb * nq
        if do_ctx:
            qs = q_stack(q0)
            s_c = _nt_dot(kctx, qs)
            m = jnp.maximum(jnp.max(s_c, axis=0, keepdims=True), sink_row)
            p_c = jnp.exp2(s_c - m)
            l = jnp.sum(p_c, axis=0, keepdims=True) + jnp.exp2(sink_row - m)
            o = jnp.dot(vctx, p_c.astype(BF16), preferred_element_type=F32)
            emit(q0, o * (1.0 / l))
        else:
            o_ref[q0:q0 + nq, :] = jnp.zeros((nq, 2 * LANES), BF16)


def _attn_b(proj, sink, n_lat, do_ctx):
    b, s_tot, _ = proj.shape
    n_ctx = s_tot - n_lat
    kern = functools.partial(_attn_b_kernel, n_lat=n_lat, n_ctx=n_ctx, do_ctx=do_ctx)
    return pl.pallas_call(
        kern,
        grid_spec=pltpu.PrefetchScalarGridSpec(
            num_scalar_prefetch=1,
            grid=(b, KVB),
            in_specs=[
                pl.BlockSpec((None, s_tot, 2 * LANES), lambda bi, j, s: (bi, 0, 6 + j)),
                pl.BlockSpec((None, s_tot, LANES), lambda bi, j, s: (bi, 0, 16 + j)),
                pl.BlockSpec((None, s_tot, LANES), lambda bi, j, s: (bi, 0, 18)),
            ],
            out_specs=pl.BlockSpec((None, s_tot, 2 * LANES), lambda bi, j, s: (bi, 0, j)),
            scratch_shapes=[
                pltpu.VMEM((LANES, s_tot), BF16),
                pltpu.VMEM((n_ctx, GB * BLOCK), F32),
                pltpu.VMEM((3 * BLOCK, GB * BLOCK), F32),
                pltpu.VMEM((n_ctx, GB * BLOCK), F32),
                pltpu.VMEM((3 * BLOCK, GB * BLOCK), F32),
            ],
        ),
        out_shape=jax.ShapeDtypeStruct((b, s_tot, HB * HEAD_DIM), BF16),
        name="attn_win",
        compiler_params=_cparams(("parallel", "parallel")),
    )(sink, proj, proj, proj)


def _route_t(logits_t, bias_col):
    s = 1.0 / (1.0 + jnp.exp(-logits_t))
    sb = s + bias_col
    s_rows = [s[e:e + 1, :] for e in range(N_EXPERTS)]
    sb_rows = [sb[e:e + 1, :] for e in range(N_EXPERTS)]
    best = None
    gsel = None
    for g in range(N_GROUPS):
        a, b, c, d = sb_rows[4 * g:4 * g + 4]
        hi1, lo1 = jnp.maximum(a, b), jnp.minimum(a, b)
        hi2, lo2 = jnp.maximum(c, d), jnp.minimum(c, d)
        gs = jnp.maximum(hi1, hi2) + jnp.maximum(jnp.minimum(hi1, hi2), jnp.maximum(lo1, lo2))
        if g == 0:
            best, gsel = gs, jnp.zeros_like(gs, dtype=jnp.int32)
        else:
            better = gs > best
            gsel = jnp.where(better, g, gsel)
            best = jnp.where(better, gs, best)
    masked = [jnp.where(gsel == (e // EXPERTS_PER_GROUP), sb_rows[e], -jnp.inf) for e in range(N_EXPERTS)]

    def argtop(vals):
        v, i = vals[0], jnp.zeros_like(gsel)
        for e in range(1, N_EXPERTS):
            better = vals[e] > v
            i = jnp.where(better, e, i)
            v = jnp.where(better, vals[e], v)
        return i

    i1 = argtop(masked)
    i2 = argtop([jnp.where(i1 == e, -jnp.inf, masked[e]) for e in range(N_EXPERTS)])
    w1 = sum(jnp.where(i1 == e, s_rows[e], 0.0) for e in range(N_EXPERTS))
    w2 = sum(jnp.where(i2 == e, s_rows[e], 0.0) for e in range(N_EXPERTS))
    inv = 1.0 / (w1 + w2)
    w1, w2 = w1 * inv, w2 * inv
    swap = i2 < i1
    lo = jnp.where(swap, i2, i1) - gsel * EXPERTS_PER_GROUP
    hi = jnp.where(swap, i1, i2) - gsel * EXPERTS_PER_GROUP
    g_lo = jnp.where(swap, w2, w1)
    g_hi = jnp.where(swap, w1, w2)
    pair = jnp.where(lo == 0, hi - 1, jnp.where(lo == 1, hi + 1, PAIRS_PER_GROUP - 1))
    return gsel * PAIRS_PER_GROUP + pair, g_lo, g_hi


def _route_kernel(lat_ref, ctx_ref, oa_ref, ob_ref, wo_ref, mod_ref, modc_ref, g_ref, wr_ref, br_ref, tri_ref,
                  xo_ref, slab_ref, info_ref, cnt_ref, base_ref, *, ctx_t0):
    n_sub, tm = lat_ref.shape[0], lat_ref.shape[1]

    @pl.when((pl.program_id(0) == 0) & (pl.program_id(1) == 0))
    def _():
        base_ref[...] = jnp.zeros_like(base_ref)

    is_ctx = pl.program_id(1) >= ctx_t0
    row128 = lax.broadcasted_iota(jnp.int32, (LANES, tm), 0)
    rowb = lax.broadcasted_iota(jnp.int32, (BUCKET_ROWS, tm), 0)
    row8 = lax.broadcasted_iota(jnp.int32, (8, tm), 0)
    slab_ref[...] = jnp.zeros_like(slab_ref)
    base = base_ref[...]
    for j in range(n_sub):
        mod = jnp.where(is_ctx, modc_ref[...], mod_ref[j])
        mix = jnp.dot(oa_ref[j], wo_ref[0:D_QA, :], preferred_element_type=F32)
        mix += jnp.dot(ob_ref[j], wo_ref[D_QA:2 * D_QA, :], preferred_element_type=F32)
        x = jnp.where(is_ctx, ctx_ref[j], lat_ref[j]) + mod[2:3, :] * mix
        xo_ref[j] = x
        h = _rms(x) * g_ref[...] * (1.0 + mod[4:5, :]) + mod[3:4, :]
        h_hi, h_lo = _split_bf16(h)
        lt = _nt_dot(wr_ref[...], h_hi)
        lt2 = _nt_dot(wr_ref[0:N_EXPERTS, :], h_lo)
        logits_t = lt[0:N_EXPERTS] + lt[N_EXPERTS:2 * N_EXPERTS] + lt2
        bucket, g_lo, g_hi = _route_t(logits_t, br_ref[...])

        gate_t = jnp.where(row128 == 0, g_lo, jnp.where(row128 == 1, g_hi, 0.0))
        for s in range(OUT_SLAB):
            slab_ref[j, pl.ds(s, tm, stride=IN_SLAB), :] = h[:, s * LANES:(s + 1) * LANES]
        slab_ref[j, pl.ds(OUT_SLAB, tm, stride=IN_SLAB), :] = gate_t.T

        member = rowb == bucket
        ranks = jnp.dot(member.astype(BF16), tri_ref[...], preferred_element_type=F32) + base[:, 0:1]
        rank = jnp.sum(jnp.where(member, ranks, 0.0), axis=0, keepdims=True)
        info_ref[j] = jnp.where(row8 == 0, bucket.astype(F32), jnp.where(row8 == 1, rank, 0.0))
        base = base + jnp.sum(member.astype(F32), axis=1, keepdims=True)
    base_ref[...] = base
    cnt_ref[...] = base


def _route(lat_src, ctx_src, ctx_off, alias_lat, oa, ob, w_out, mod, gnorm, wr_t, br_col, n_lat, n_rows, s_tot):
    b = lat_src.shape[0]
    tm = ROUTE_TM
    g = ROUTE_SAMPLES
    n_t = n_rows // tm
    ctx_t0 = n_lat // tm
    tri = (jnp.arange(tm)[:, None] < jnp.arange(tm)[None, :]).astype(BF16)
    const = lambda shape: pl.BlockSpec(shape, lambda bi, t: (0,) * len(shape))
    tile = lambda width: pl.BlockSpec((g, tm, width), lambda bi, t: (bi, t, 0))
    xs, slabs, info, counts = pl.pallas_call(
        functools.partial(_route_kernel, ctx_t0=ctx_t0),
        grid=(b // g, n_t),
        in_specs=[
            *_token_specs(tm, ctx_t0, ctx_off, g),
            tile(D_QA), tile(D_QA),
            const((D_MODEL, D_MODEL)),
            pl.BlockSpec((g, 6, D_MODEL), lambda bi, t: (bi, 0, 0)),
            pl.BlockSpec((None, 6, D_MODEL), lambda bi, t: (b, 0, 0)),
            const((1, D_MODEL)),
            const((2 * N_EXPERTS, D_MODEL)),
            const((N_EXPERTS, 1)),
            const((tm, tm)),
        ],
        out_specs=[
            tile(D_MODEL),
            pl.BlockSpec((g, tm * IN_SLAB, LANES), lambda bi, t: (bi, t, 0)),
            pl.BlockSpec((g, 8, tm), lambda bi, t: (bi, 0, t)),
            const((BUCKET_ROWS, LANES)),
        ],
        out_shape=[
            jax.ShapeDtypeStruct((b, s_tot, D_MODEL), F32),
            jax.ShapeDtypeStruct((b, n_t * tm * IN_SLAB, LANES), F32),
            jax.ShapeDtypeStruct((b, 8, n_t * tm), F32),
            jax.ShapeDtypeStruct((BUCKET_ROWS, LANES), F32),
        ],
        scratch_shapes=[pltpu.VMEM((BUCKET_ROWS, LANES), F32)],
        input_output_aliases={0: 0} if alias_lat else {},
        name="mix_route",
        compiler_params=_cparams(("arbitrary", "arbitrary")),
    )(lat_src, ctx_src, oa, ob, w_out, mod, mod, gnorm, wr_t, br_col, tri)
    return xs, slabs.reshape(b * n_t * tm * IN_SLAB, LANES), info, counts


def _row_copy(src, dst, sem):
    return pltpu.make_async_copy(src, dst, sem)


def _slab(ref, i, n):
    return ref.at[pl.ds(pl.multiple_of(i * n, n), n), :]


def _slab_rows(ref, s, tm, n):
    return ref[pl.ds(s, tm, stride=n), :]


def _permute_kernel(ends_ref, pos_ref, slab_ref, xs_hbm, zero_ref, sem, zsem):
    tm = slab_ref.shape[0] // IN_SLAB
    tile = MOE_TMG * IN_SLAB

    @pl.when((pl.program_id(0) == 0) & (pl.program_id(1) == 0))
    def _():
        zero_ref[...] = jnp.zeros_like(zero_ref)

        def zero_tile(first_token):
            dst = xs_hbm.at[pl.ds(pl.multiple_of(first_token * IN_SLAB, tile), tile), :]
            cp = _row_copy(zero_ref, dst, zsem)
            cp.start()
            cp.wait()

        def pad_fill(g, c):
            prev = ends_ref[jnp.maximum(g - 1, 0)]
            size = ends_ref[g] - jnp.where(g == 0, 0, prev)

            @pl.when(size > 0)
            def _():
                zero_tile(ends_ref[g] - MOE_TMG)
            return c

        lax.fori_loop(0, N_BUCKETS, pad_fill, 0)

        n_tiles = xs_hbm.shape[0] // tile

        def tail_fill(t0, c):
            @pl.when(t0 * MOE_TMG >= ends_ref[N_BUCKETS - 1])
            def _():
                zero_tile(t0 * MOE_TMG)
            return c

        lax.fori_loop(n_tiles - N_BUCKETS, n_tiles, tail_fill, 0)

    def body(r, c):
        _row_copy(_slab(slab_ref, r, IN_SLAB), _slab(xs_hbm, pos_ref[0, 0, r], IN_SLAB), sem).start()
        return c

    lax.fori_loop(0, tm, body, 0, unroll=8)
    _row_copy(slab_ref, xs_hbm.at[pl.ds(0, tm * IN_SLAB), :], sem).wait()


def _permute(slabs, pos, ends, cap, tm):
    n_t = pos.shape[0]
    return pl.pallas_call(
        _permute_kernel,
        grid_spec=pltpu.PrefetchScalarGridSpec(
            num_scalar_prefetch=1,
            grid=(1, n_t),
            in_specs=[
                pl.BlockSpec((1, 1, tm), lambda bi, t, e: (t, 0, 0), memory_space=pltpu.SMEM),
                pl.BlockSpec((tm * IN_SLAB, LANES), lambda bi, t, e: (t, 0)),
            ],
            out_specs=pl.BlockSpec(memory_space=pl.ANY),
            scratch_shapes=[
                pltpu.VMEM((MOE_TMG * IN_SLAB, LANES), F32),
                pltpu.SemaphoreType.DMA(()),
                pltpu.SemaphoreType.DMA(()),
            ],
        ),
        out_shape=jax.ShapeDtypeStruct((cap * IN_SLAB, LANES), F32),
        name="moe_permute",
        compiler_params=_cparams(("arbitrary", "arbitrary")),
    )(ends, pos, slabs)


def _ffn_kernel(ea_ref, eb_ref, val_ref, x_ref, w1a_ref, w3a_ref, w2a_ref, w1b_ref, w3b_ref, w2b_ref, o_ref):
    i = pl.program_id(0)
    tm = x_ref.shape[0] // IN_SLAB

    @pl.when(val_ref[i] > 0)
    def _():
        x = jnp.concatenate([_slab_rows(x_ref, s, tm, IN_SLAB).astype(BF16) for s in range(OUT_SLAB)],
                            axis=1)
        gates = _slab_rows(x_ref, OUT_SLAB, tm, IN_SLAB)
        acc = None
        for k, (w1_ref, w3_ref, w2_ref) in enumerate(((w1a_ref, w3a_ref, w2a_ref), (w1b_ref, w3b_ref, w2b_ref))):
            a = jnp.dot(x, w1_ref[...], preferred_element_type=F32)
            b = jnp.dot(x, w3_ref[...], preferred_element_type=F32)
            u = (a * (1.0 / (1.0 + jnp.exp(-a)))) * b * gates[:, k:k + 1]
            y = jnp.dot(u.astype(BF16), w2_ref[...], preferred_element_type=F32)
            acc = y if acc is None else acc + y
        for s in range(OUT_SLAB):
            o_ref[pl.ds(s, tm, stride=OUT_SLAB), :] = acc[:, s * LANES:(s + 1) * LANES]

    @pl.when(val_ref[i] == 0)
    def _():
        o_ref[...] = jnp.zeros_like(o_ref)


def _ffn(xs_sorted, tile_ea, tile_eb, tile_valid, w1, w3, w2, layer):
    n_tiles = xs_sorted.shape[0] // (MOE_TMG * IN_SLAB)
    up = lambda sel: pl.BlockSpec((None, None, D_MODEL, D_EXPERT),
                                  lambda i, ea, eb, v: (layer, sel(ea, eb)[i], 0, 0))
    down = lambda sel: pl.BlockSpec((None, None, D_EXPERT, D_MODEL),
                                    lambda i, ea, eb, v: (layer, sel(ea, eb)[i], 0, 0))
    first = lambda ea, eb: ea
    second = lambda ea, eb: eb
    return pl.pallas_call(
        _ffn_kernel,
        grid_spec=pltpu.PrefetchScalarGridSpec(
            num_scalar_prefetch=3,
            grid=(n_tiles,),
            in_specs=[
                pl.BlockSpec((MOE_TMG * IN_SLAB, LANES), lambda i, ea, eb, v: (i, 0)),
                up(first), up(first), down(first),
                up(second), up(second), down(second),
            ],
            out_specs=pl.BlockSpec((MOE_TMG * OUT_SLAB, LANES), lambda i, ea, eb, v: (i, 0)),
        ),
        out_shape=jax.ShapeDtypeStruct((n_tiles * MOE_TMG * OUT_SLAB, LANES), F32),
        name="moe_ffn",
        compiler_params=_cparams(("arbitrary",)),
    )(tile_ea, tile_eb, tile_valid, xs_sorted, w1, w3, w2, w1, w3, w2)


def _combine_kernel(pos_ref, x_ref, mod_ref, modc_ref, gfin_ref, ys_hbm, o_ref, buf_ref, sem, *, n_lat, final):
    tm = x_ref.shape[0]

    def body(r, c):
        _row_copy(_slab(ys_hbm, pos_ref[0, 0, r], OUT_SLAB), _slab(buf_ref, r, OUT_SLAB), sem).start()
        return c

    lax.fori_loop(0, tm, body, 0, unroll=8)
    _row_copy(ys_hbm.at[pl.ds(0, tm * OUT_SLAB), :], buf_ref, sem).wait()
    y = jnp.concatenate([_slab_rows(buf_ref, s, tm, OUT_SLAB) for s in range(OUT_SLAB)], axis=1)
    row = pl.program_id(1) * tm + lax.broadcasted_iota(jnp.int32, (tm, 1), 0)
    gate = jnp.where(row >= n_lat, modc_ref[5:6, :], mod_ref[5:6, :])
    out = x_ref[...] + gate * y
    if final:
        out = _rms(out) * gfin_ref[...]
    o_ref[...] = out


def _combine(xs, ys, pos, mod, gfin, n_lat, n_rows, tm, final):
    b, s_tot, _ = xs.shape
    n_t = n_rows // tm
    x_spec = pl.BlockSpec((None, tm, D_MODEL), lambda bi, t: (bi, t, 0))
    return pl.pallas_call(
        functools.partial(_combine_kernel, n_lat=n_lat, final=final),
        grid=(b, n_t),
        in_specs=[
            pl.BlockSpec((1, 1, tm), lambda bi, t: (bi * n_t + t, 0, 0), memory_space=pltpu.SMEM),
            x_spec,
            pl.BlockSpec((None, 6, D_MODEL), lambda bi, t: (bi, 0, 0)),
            pl.BlockSpec((None, 6, D_MODEL), lambda bi, t: (b, 0, 0)),
            pl.BlockSpec((1, D_MODEL), lambda bi, t: (0, 0)),
            pl.BlockSpec(memory_space=pl.ANY),
        ],
        out_specs=x_spec,
        out_shape=jax.ShapeDtypeStruct((b, n_rows, D_MODEL) if final else xs.shape, F32),
        scratch_shapes=[pltpu.VMEM((tm * OUT_SLAB, LANES), F32), pltpu.SemaphoreType.DMA(())],
        input_output_aliases={} if final else {1: 0},
        name="moe_combine",
        compiler_params=_cparams(("arbitrary", "arbitrary")),
    )(pos, xs, mod, mod, gfin, ys)


def _moe(xs, rows, info, counts, mod, w1, w3, w2, gfin, *, layer, n_lat, n_rows, final):
    b = xs.shape[0]
    n_tok = b * n_rows

    tmg = MOE_TMG
    cap = (n_tok // tmg + N_BUCKETS) * tmg
    cnt = counts[:N_BUCKETS, 0].astype(jnp.int32)
    padded = (cnt + tmg - 1) // tmg * tmg
    ends = jnp.cumsum(padded)
    starts = ends - padded
    bucket = info[:, 0, :].astype(jnp.int32)
    start_of = sum(jnp.where(bucket == k, starts[k], 0) for k in range(N_BUCKETS))
    pos = start_of + info[:, 1, :].astype(jnp.int32)
    tm_p = next(t for t in PERMUTE_TMS if n_tok % t == 0)
    tm_c = n_rows // COMBINE_TILES
    tile_start = jnp.arange(cap // tmg, dtype=jnp.int32) * tmg
    tile_valid = (tile_start < ends[-1]).astype(jnp.int32)
    tile_bkt = sum((tile_start >= ends[k]).astype(jnp.int32) for k in range(N_BUCKETS - 1))
    last_bkt = jnp.max(jnp.where(tile_valid > 0, tile_bkt, 0))
    tile_bkt = jnp.where(tile_valid > 0, tile_bkt, last_bkt)
    pair_lo = (0, 0, 0, 1, 1, 2)
    pair_hi = (1, 2, 3, 2, 3, 3)
    grp, pair = tile_bkt // PAIRS_PER_GROUP, tile_bkt % PAIRS_PER_GROUP
    tile_ea = grp * EXPERTS_PER_GROUP + sum(jnp.where(pair == p, pair_lo[p], 0) for p in range(PAIRS_PER_GROUP))
    tile_eb = grp * EXPERTS_PER_GROUP + sum(jnp.where(pair == p, pair_hi[p], 0) for p in range(PAIRS_PER_GROUP))

    xs_sorted = _permute(rows, pos.reshape(n_tok // tm_p, 1, tm_p), ends.astype(jnp.int32), cap, tm_p)
    ys = _ffn(xs_sorted, tile_ea.astype(jnp.int32), tile_eb.astype(jnp.int32), tile_valid, w1, w3, w2, layer)
    return _combine(xs, ys, pos.reshape(n_tok // tm_c, 1, tm_c), mod, gfin, n_lat, n_rows, tm_c, final)


def _rope_tables(n_lat, n_ctx):
    pos = jnp.arange(n_lat)
    row = (pos // GRID_W).astype(F32)
    col = (pos % GRID_W).astype(F32)
    n_freq = HEAD_DIM // 4
    inv = 1.0 / (ROPE_BASE ** (jnp.arange(n_freq, dtype=F32) / n_freq))
    lane = jnp.arange(LANES) % HEAD_DIM
    axis = lane // 32
    second = (lane % 32) // 16
    freq = inv[lane % 16]
    ang = jnp.where(axis[None, :] == 0, row[:, None], col[:, None]) * freq[None, :]
    cos = jnp.cos(ang)
    sin = jnp.sin(ang) * jnp.where(second == 0, -1.0, 1.0)[None, :]
    cos = jnp.concatenate([cos, jnp.ones((n_ctx, LANES), F32)], axis=0)
    sin = jnp.concatenate([sin, jnp.zeros((n_ctx, LANES), F32)], axis=0)
    q_scale = HEAD_DIM ** -0.5 * LOG2E
    return cos * q_scale, sin * q_scale, cos, sin


def _proj_cols():
    base = jnp.arange(D_IN)
    kb0 = base[2048:2112]
    kb1 = base[2112:2176]
    return jnp.concatenate([base[:2048], kb0, kb0, kb1, kb1, base[2176:]])


def kernel(x, c, ctx, c_ctx, w_ada, b_ada, norm_attn, norm_ffn, w_in, w_out, lambda_qk, subln, sink, w_router,
           b_router, w1, w3, w2, norm_final):
    b, n_lat, _ = x.shape
    n_ctx = ctx.shape[1]
    depth = w_in.shape[0]

    s_tot = n_lat + n_ctx
    c_all = jnp.zeros((MOD_ROWS, D_MODEL), F32).at[:b].set(c).at[b].set(c_ctx)
    mod_all = _ada(c_all, w_ada, b_ada).reshape(depth, MOD_ROWS, 6, D_MODEL)
    tables = _rope_tables(n_lat, n_ctx)
    cols = _proj_cols()

    wr_hi, wr_lo = _split_bf16(w_router.T)
    wr_t = jnp.concatenate([wr_hi, wr_lo], axis=0)
    br_col = b_router.reshape(N_EXPERTS, 1)
    gfin = norm_final.reshape(1, D_MODEL)

    w1b, w3b, w2b = w1.astype(BF16), w3.astype(BF16), w2.astype(BF16)
    ctx_t0 = n_lat // PROJ_TM

    lat_src, ctx_src, ctx_off = x, ctx, 0
    out = None
    for l in range(depth):
        last = l == depth - 1
        lam_init = 0.8 - 0.6 * math.exp(-0.3 * l)
        mod = mod_all[l]
        w_in_l = w_in[l][:, cols].astype(BF16)
        proj = _proj(lat_src, ctx_src, ctx_off, mod, norm_attn[l].reshape(1, D_MODEL), w_in_l, tables, n_lat, s_tot)
        oa = _attn_a(proj, lambda_qk[l], subln[l].reshape(2 * HEAD_DIM, 1), n_lat, not last, lam_init)
        ob = _attn_b(proj, sink[l], n_lat, not last)
        n_rows = n_lat if last else s_tot
        in_place = l > 0
        xs, rows, info, counts = _route(lat_src, ctx if in_place else ctx_src, 0 if in_place else ctx_off, in_place,
                                        oa, ob, w_out[l].astype(BF16), mod, norm_ffn[l].reshape(1, D_MODEL),
                                        wr_t, br_col, n_lat, n_rows, s_tot)
        res = _moe(xs, rows, info, counts, mod, w1b, w3b, w2b, gfin, layer=l, n_lat=n_lat, n_rows=n_rows,
                   final=last)
        if last:
            out = res
        else:
            lat_src, ctx_src, ctx_off = res, res, ctx_t0
    return out
```

```python
import functools
import math

import jax
import jax.numpy as jnp
from jax import lax
from jax.experimental import pallas as pl
from jax.experimental.pallas import tpu as pltpu

F32 = jnp.float32
BF16 = jnp.bfloat16

D_MODEL = 1024
HEAD_DIM = 64
HA = 4
HB = 8
KVB = 2
GB = HB // KVB
GRID_W = 64
WINDOW = 128
BLOCK = 128
ROPE_BASE = 10000.0
N_EXPERTS = 16
N_GROUPS = 4
EXPERTS_PER_GROUP = N_EXPERTS // N_GROUPS
D_EXPERT = 512
EPS = 1e-6
NEG = -1e30
LOG2E = 1.4426950408889634

LANES = 128
D_QA = HA * 2 * HEAD_DIM
D_IN = 2304
N_PROJ_TILES = 19
D_PROJ = N_PROJ_TILES * LANES
Q_TILES = (0, 1, 2, 3, 12, 13, 14, 15)
K_TILES = (4, 5, 6, 7, 16, 17)

VMEM_LIMIT = 56 * 1024 * 1024

PROJ_TM = 256
ATTN_TQ = 1024
ATTN_KC = 512
ROUTE_TM = 256
ROUTE_SAMPLES = 2
PAIRS_PER_GROUP = 6
N_BUCKETS = N_GROUPS * PAIRS_PER_GROUP
BUCKET_ROWS = 32
MOE_TMG = 256
PERMUTE_TMS = (2048, 1024, 512, 256)
COMBINE_TILES = 4
OUT_SLAB = D_MODEL // LANES
IN_SLAB = 2 * OUT_SLAB
MOD_ROWS = 16


def _cparams(sem):
    return pltpu.CompilerParams(dimension_semantics=sem, vmem_limit_bytes=VMEM_LIMIT)


def _nt_dot(a, b):
    return lax.dot_general(a, b, (((1,), (1,)), ((), ())), preferred_element_type=F32)


def _split_bf16(x):
    hi = x.astype(BF16)
    lo = (x - hi.astype(F32)).astype(BF16)
    return hi, lo


def _rms(x):
    return x * lax.rsqrt(jnp.mean(x * x, axis=-1, keepdims=True) + EPS)


def _ada_kernel(c_ref, w_ref, b_ref, o_ref):
    c = c_ref[...]
    a = c * (1.0 / (1.0 + jnp.exp(-c)))
    a_hi, a_lo = _split_bf16(a)
    w_hi, w_lo = _split_bf16(w_ref[...])
    acc = jnp.dot(a_hi, w_hi, preferred_element_type=F32)
    acc += jnp.dot(a_hi, w_lo, preferred_element_type=F32)
    acc += jnp.dot(a_lo, w_hi, preferred_element_type=F32)
    o_ref[...] = acc + b_ref[...]


def _ada(c_all, w_ada, b_ada):
    depth = w_ada.shape[0]
    n_out = w_ada.shape[2]
    tn = D_MODEL
    return pl.pallas_call(
        _ada_kernel,
        grid=(depth, n_out // tn),
        in_specs=[
            pl.BlockSpec((MOD_ROWS, D_MODEL), lambda l, n: (0, 0)),
            pl.BlockSpec((None, D_MODEL, tn), lambda l, n: (l, 0, n)),
            pl.BlockSpec((None, 1, tn), lambda l, n: (l, 0, n)),
        ],
        out_specs=pl.BlockSpec((None, MOD_ROWS, tn), lambda l, n: (l, 0, n)),
        out_shape=jax.ShapeDtypeStruct((depth, MOD_ROWS, n_out), F32),
        name="ada_mod",
        compiler_params=_cparams(("parallel", "parallel")),
    )(c_all, w_ada, b_ada.reshape(depth, 1, n_out))


def _token_specs(tm, ctx_t0, ctx_off, samples=None):
    lat = pl.BlockSpec((samples, tm, D_MODEL), lambda bi, t, *_: (bi, jnp.minimum(t, ctx_t0 - 1), 0))
    ctx = pl.BlockSpec((samples, tm, D_MODEL), lambda bi, t, *_: (bi, ctx_off + jnp.maximum(t - ctx_t0, 0), 0))
    return lat, ctx


def _proj_kernel(lat_ref, ctx_ref, mod_ref, g_ref, w_ref, cq_ref, sq_ref, ck_ref, sk_ref, o_ref, *, ctx_t0):
    x = jnp.where(pl.program_id(1) >= ctx_t0, ctx_ref[...], lat_ref[...])
    h = _rms(x) * g_ref[...] * (1.0 + mod_ref[1:2, :]) + mod_ref[0:1, :]
    acc = jnp.dot(h.astype(BF16), w_ref[...], preferred_element_type=F32)
    tm = x.shape[0]
    lane = lax.broadcasted_iota(jnp.int32, (tm, LANES), 1)
    first_half = (lane % 32) < 16
    for t in range(N_PROJ_TILES):
        a = acc[:, t * LANES:(t + 1) * LANES]
        if t in Q_TILES or t in K_TILES:
            cos, sin = (cq_ref, sq_ref) if t in Q_TILES else (ck_ref, sk_ref)
            partner = jnp.where(first_half, pltpu.roll(a, LANES - 16, 1), pltpu.roll(a, 16, 1))
            a = a * cos[...] + partner * sin[...]
        o_ref[:, t * LANES:(t + 1) * LANES] = a.astype(BF16)


def _proj(lat_src, ctx_src, ctx_off, mod, gnorm, w, tables, n_lat, s_tot):
    b = lat_src.shape[0]
    tm = PROJ_TM
    n_t = s_tot // tm
    ctx_t0 = n_lat // tm
    tab_spec = pl.BlockSpec((tm, LANES), lambda bi, t: (t, 0))
    return pl.pallas_call(
        functools.partial(_proj_kernel, ctx_t0=ctx_t0),
        grid=(b, n_t),
        in_specs=[
            *_token_specs(tm, ctx_t0, ctx_off),
            pl.BlockSpec((None, 6, D_MODEL), lambda bi, t: (jnp.where(t >= ctx_t0, b, bi), 0, 0)),
            pl.BlockSpec((1, D_MODEL), lambda bi, t: (0, 0)),
            pl.BlockSpec((D_MODEL, D_PROJ), lambda bi, t: (0, 0)),
            tab_spec, tab_spec, tab_spec, tab_spec,
        ],
        out_specs=pl.BlockSpec((None, tm, D_PROJ), lambda bi, t: (bi, t, 0)),
        out_shape=jax.ShapeDtypeStruct((b, s_tot, D_PROJ), BF16),
        name="in_proj",
        compiler_params=_cparams(("parallel", "parallel")),
    )(lat_src, ctx_src, mod, gnorm, w, *tables)


def _transpose_into(vt_ref, v_ref, s_tot):
    for c in range(s_tot // LANES):
        blk = v_ref[c * LANES:(c + 1) * LANES, :].astype(F32)
        vt_ref[:, c * LANES:(c + 1) * LANES] = blk.T.astype(BF16)


def _attn_a_kernel(q_ref, k_ref, v_ref, lam_ref, subln_ref, o_ref, vt_ref, acc1_ref, acc2_ref,
                   s0_ref, s1_ref, *, n_lat, n_ctx, do_ctx, lam_init):
    s_tot = n_lat + n_ctx
    _transpose_into(vt_ref, v_ref, s_tot)

    lq = lam_ref[...]
    lam = (jnp.exp(jnp.sum(lq[0:1] * lq[1:2], axis=1, keepdims=True))
           - jnp.exp(jnp.sum(lq[2:3] * lq[3:4], axis=1, keepdims=True)) + lam_init)
    lane = lax.broadcasted_iota(jnp.int32, (1, LANES), 1)
    m_lo = (lane < HEAD_DIM).astype(BF16)
    m_hi = (lane >= HEAD_DIM).astype(BF16)
    subln = subln_ref[...]

    def finish(o1, l1, o2, l2):
        o = o1 * (1.0 / l1) - lam * (o2 * (1.0 / l2))
        o = o * lax.rsqrt(jnp.mean(o * o, axis=0, keepdims=True) + EPS)
        o = o * subln * (1.0 - lam_init)
        return o.T.astype(BF16)

    def one_shot(ks, vt, qm):
        s = _nt_dot(ks, qm)
        m = jnp.max(s, axis=0, keepdims=True)
        p = jnp.exp2(s - m)
        l = jnp.sum(p, axis=0, keepdims=True)
        return jnp.dot(vt, p.astype(BF16), preferred_element_type=F32), m, l

    tq, kc = ATTN_TQ, ATTN_KC
    n_kv = n_lat // kc

    def q_tile(i, carry):
        q0 = pl.multiple_of(i * tq, tq)
        qs = q_ref[pl.ds(q0, tq), :]
        qm = (qs * m_lo, qs * m_hi)

        def score(k0, size, dst_ref):
            ks = k_ref[pl.ds(k0, size), :]
            maxes = []
            for mp in range(2):
                s = _nt_dot(ks, qm[mp])
                dst_ref[mp, 0:size, :] = s
                maxes.append(jnp.max(s, axis=0, keepdims=True))
            return tuple(maxes)

        def update(src_ref, smaxes, k0, size, c):
            vt = vt_ref[:, pl.ds(k0, size)]
            out = []
            for mp, acc_ref in ((0, acc1_ref), (1, acc2_ref)):
                s = src_ref[mp, 0:size, :]
                smax = smaxes[mp]
                if c is None:
                    m_new = smax
                    p = jnp.exp2(s - m_new)
                    l_new = jnp.sum(p, axis=0, keepdims=True)
                    acc_ref[...] = jnp.dot(vt, p.astype(BF16), preferred_element_type=F32)
                else:
                    m, l = c[2 * mp], c[2 * mp + 1]
                    m_new = jnp.maximum(m, smax)
                    alpha = jnp.exp2(m - m_new)
                    p = jnp.exp2(s - m_new)
                    l_new = alpha * l + jnp.sum(p, axis=0, keepdims=True)
                    acc_ref[...] = alpha * acc_ref[...] + jnp.dot(vt, p.astype(BF16),
                                                                  preferred_element_type=F32)
                out += [m_new, l_new]
            return tuple(out)

        mx0 = score(0, kc, s0_ref)
        mx1 = score(kc, kc, s1_ref)
        c = update(s0_ref, mx0, 0, kc, None)

        def pair(t, st):
            c, mx1 = st[:4], st[4:]
            ka = pl.multiple_of((2 * t + 1) * kc, kc)
            kb = pl.multiple_of(ka + kc, kc)
            mx0 = score(kb, kc, s0_ref)
            c = update(s1_ref, mx1, ka, kc, c)
            mx1 = score(pl.multiple_of(kb + kc, kc), kc, s1_ref)
            return update(s0_ref, mx0, kb, kc, c) + mx1

        st = lax.fori_loop(0, n_kv // 2 - 2, pair, c + mx1)
        c, mx1 = st[:4], st[4:]
        last = (n_kv - 1) * kc
        mx0 = score(last - kc, kc, s0_ref)
        c = update(s1_ref, mx1, last - 2 * kc, kc, c)
        mx1 = score(last, kc + n_ctx, s1_ref)
        c = update(s0_ref, mx0, last - kc, kc, c)
        m1, l1, m2, l2 = update(s1_ref, mx1, last, kc + n_ctx, c)
        o_ref[pl.ds(q0, tq), :] = finish(acc1_ref[...], l1, acc2_ref[...], l2)
        return carry

    lax.fori_loop(0, n_lat // tq, q_tile, 0)

    if do_ctx:
        qs = q_ref[n_lat:s_tot, :]
        kctx = k_ref[n_lat:s_tot, :]
        vctx = vt_ref[:, n_lat:s_tot]
        o1, _, l1 = one_shot(kctx, vctx, qs * m_lo)
        o2, _, l2 = one_shot(kctx, vctx, qs * m_hi)
        o_ref[n_lat:s_tot, :] = finish(o1, l1, o2, l2)
    else:
        o_ref[n_lat:s_tot, :] = jnp.zeros((n_ctx, LANES), BF16)


def _attn_a(proj, lam_qk, subln_col, n_lat, do_ctx, lam_init):
    b, s_tot, _ = proj.shape
    n_ctx = s_tot - n_lat
    n_kv = n_lat // ATTN_KC
    assert n_lat % ATTN_TQ == 0 and n_lat % ATTN_KC == 0 and n_kv >= 4 and n_kv % 2 == 0, n_lat
    kern = functools.partial(_attn_a_kernel, n_lat=n_lat, n_ctx=n_ctx, do_ctx=do_ctx, lam_init=lam_init)
    blk = lambda off: pl.BlockSpec((None, s_tot, LANES), lambda bi, h: (bi, 0, off + h))
    return pl.pallas_call(
        kern,
        grid=(b, HA),
        in_specs=[
            blk(0), blk(4), blk(8),
            pl.BlockSpec((4, HEAD_DIM), lambda bi, h: (0, 0)),
            pl.BlockSpec((2 * HEAD_DIM, 1), lambda bi, h: (0, 0)),
        ],
        out_specs=pl.BlockSpec((None, s_tot, LANES), lambda bi, h: (bi, 0, h)),
        out_shape=jax.ShapeDtypeStruct((b, s_tot, HA * LANES), BF16),
        scratch_shapes=[
            pltpu.VMEM((LANES, s_tot), BF16),
            pltpu.VMEM((LANES, ATTN_TQ), F32),
            pltpu.VMEM((LANES, ATTN_TQ), F32),
            pltpu.VMEM((2, ATTN_KC, ATTN_TQ), F32),
            pltpu.VMEM((2, ATTN_KC + n_ctx, ATTN_TQ), F32),
        ],
        name="attn_diff",
        compiler_params=_cparams(("parallel", "parallel")),
    )(proj, proj, proj, lam_qk, subln_col)


def _attn_b_kernel(sink_ref, q_ref, k_ref, v_ref, o_ref, vt_ref, sc0_ref, sw0_ref, sc1_ref, sw1_ref,
                   *, n_lat, n_ctx, do_ctx):
    s_tot = n_lat + n_ctx
    j = pl.program_id(1)
    _transpose_into(vt_ref, v_ref, s_tot)

    lane = lax.broadcasted_iota(jnp.int32, (1, LANES), 1)
    m_lo = (lane < HEAD_DIM).astype(BF16)
    m_hi = (lane >= HEAD_DIM).astype(BF16)
    nq = BLOCK
    win = 3 * BLOCK
    sink_row = jnp.concatenate(
        [jnp.full((1, nq), sink_ref[j * GB + g] * LOG2E, F32) for g in range(GB)], axis=1)
    rc = (lax.broadcasted_iota(jnp.int32, (win, nq), 0) - lax.broadcasted_iota(jnp.int32, (win, nq), 1))
    v0 = pl.multiple_of(j * HEAD_DIM, HEAD_DIM)

    def q_stack(q0):
        qa = q_ref[pl.ds(q0, nq), 0:LANES]
        qb = q_ref[pl.ds(q0, nq), LANES:2 * LANES]
        return jnp.concatenate([qa * m_lo, qa * m_hi, qb * m_lo, qb * m_hi], axis=0)

    def emit(q0, o):
        for pr in range(2):
            pair = jnp.concatenate([o[:, (2 * pr) * nq:(2 * pr + 1) * nq],
                                    o[:, (2 * pr + 1) * nq:(2 * pr + 2) * nq]], axis=0)
            o_ref[pl.ds(q0, nq), pr * LANES:(pr + 1) * LANES] = pair.T.astype(BF16)

    kctx = k_ref[n_lat:s_tot, :]
    vctx = vt_ref[pl.ds(v0, HEAD_DIM), n_lat:s_tot]

    def win_start(i):
        return pl.multiple_of(jnp.clip((i - 1) * nq, 0, n_lat - win), nq)

    def score(i, sc_ref, sw_ref):
        qs = q_stack(pl.multiple_of(i * nq, nq))
        start = win_start(i)
        sc_ref[...] = _nt_dot(kctx, qs)
        s_w = _nt_dot(k_ref[pl.ds(start, win), :], qs)
        rel = rc + (start - i * nq)
        ok = (rel <= WINDOW) & (rel >= -WINDOW)
        sw_ref[...] = jnp.where(jnp.concatenate([ok] * GB, axis=1), s_w, NEG)

    def attend(i, sc_ref, sw_ref):
        vw = vt_ref[pl.ds(v0, HEAD_DIM), pl.ds(win_start(i), win)]
        s_c = sc_ref[...]
        s_w = sw_ref[...]
        m = jnp.maximum(jnp.maximum(jnp.max(s_c, axis=0, keepdims=True),
                                    jnp.max(s_w, axis=0, keepdims=True)), sink_row)
        p_c = jnp.exp2(s_c - m)
        p_w = jnp.exp2(s_w - m)
        l = (jnp.sum(p_c, axis=0, keepdims=True) + jnp.sum(p_w, axis=0, keepdims=True)
             + jnp.exp2(sink_row - m))
        o = (jnp.dot(vctx, p_c.astype(BF16), preferred_element_type=F32)
             + jnp.dot(vw, p_w.astype(BF16), preferred_element_type=F32))
        emit(pl.multiple_of(i * nq, nq), o * (1.0 / l))

    n_blk = n_lat // nq
    score(0, sc0_ref, sw0_ref)

    def pair(t, carry):
        score(2 * t + 1, sc1_ref, sw1_ref)
        attend(2 * t, sc0_ref, sw0_ref)
        score(2 * t + 2, sc0_ref, sw0_ref)
        attend(2 * t + 1, sc1_ref, sw1_ref)
        return carry

    lax.fori_loop(0, n_blk // 2 - 1, pair, 0)
    score(n_blk - 1, sc1_ref, sw1_ref)
    attend(n_blk - 2, sc0_ref, sw0_ref)
    attend(n_blk - 1, sc1_ref, sw1_ref)

    for cb in range(n_ctx // nq):
        q0 = n_lat + cb * nq
        if do_ctx:
            qs = q_stack(q0)
            s_c = _nt_dot(kctx, qs)
            m = jnp.maximum(jnp.max(s_c, axis=0, keepdims=True), sink_row)
            p_c = jnp.exp2(s_c - m)
            l = jnp.sum(p_c, axis=0, keepdims=True) + jnp.exp2(sink_row - m)
            o = jnp.dot(vctx, p_c.astype(BF16), preferred_element_type=F32)
            emit(q0, o * (1.0 / l))
        else:
            o_ref[q0:q0 + nq, :] = jnp.zeros((nq, 2 * LANES), BF16)


def _attn_b(proj, sink, n_lat, do_ctx):
    b, s_tot, _ = proj.shape
    n_ctx = s_tot - n_lat
    kern = functools.partial(_attn_b_kernel, n_lat=n_lat, n_ctx=n_ctx, do_ctx=do_ctx)
    return pl.pallas_call(
        kern,
        grid_spec=pltpu.PrefetchScalarGridSpec(
            num_scalar_prefetch=1,
            grid=(b, KVB),
            in_specs=[
                pl.BlockSpec((None, s_tot, 2 * LANES), lambda bi, j, s: (bi, 0, 6 + j)),
                pl.BlockSpec((None, s_tot, LANES), lambda bi, j, s: (bi, 0, 16 + j)),
                pl.BlockSpec((None, s_tot, LANES), lambda bi, j, s: (bi, 0, 18)),
            ],
            out_specs=pl.BlockSpec((None, s_tot, 2 * LANES), lambda bi, j, s: (bi, 0, j)),
            scratch_shapes=[
                pltpu.VMEM((LANES, s_tot), BF16),
                pltpu.VMEM((n_ctx, GB * BLOCK), F32),
                pltpu.VMEM((3 * BLOCK, GB * BLOCK), F32),
                pltpu.VMEM((n_ctx, GB * BLOCK), F32),
                pltpu.VMEM((3 * BLOCK, GB * BLOCK), F32),
            ],
        ),
        out_shape=jax.ShapeDtypeStruct((b, s_tot, HB * HEAD_DIM), BF16),
        name="attn_win",
        compiler_params=_cparams(("parallel", "parallel")),
    )(sink, proj, proj, proj)


def _route_t(logits_t, bias_col):
    s = 1.0 / (1.0 + jnp.exp(-logits_t))
    sb = s + bias_col
    s_rows = [s[e:e + 1, :] for e in range(N_EXPERTS)]
    sb_rows = [sb[e:e + 1, :] for e in range(N_EXPERTS)]
    best = None
    gsel = None
    for g in range(N_GROUPS):
        a, b, c, d = sb_rows[4 * g:4 * g + 4]
        hi1, lo1 = jnp.maximum(a, b), jnp.minimum(a, b)
        hi2, lo2 = jnp.maximum(c, d), jnp.minimum(c, d)
        gs = jnp.maximum(hi1, hi2) + jnp.maximum(jnp.minimum(hi1, hi2), jnp.maximum(lo1, lo2))
        if g == 0:
            best, gsel = gs, jnp.zeros_like(gs, dtype=jnp.int32)
        else:
            better = gs > best
            gsel = jnp.where(better, g, gsel)
            best = jnp.where(better, gs, best)
    masked = [jnp.where(gsel == (e // EXPERTS_PER_GROUP), sb_rows[e], -jnp.inf) for e in range(N_EXPERTS)]

    def argtop(vals):
        v, i = vals[0], jnp.zeros_like(gsel)
        for e in range(1, N_EXPERTS):
            better = vals[e] > v
            i = jnp.where(better, e, i)
            v = jnp.where(better, vals[e], v)
        return i

    i1 = argtop(masked)
    i2 = argtop([jnp.where(i1 == e, -jnp.inf, masked[e]) for e in range(N_EXPERTS)])
    w1 = sum(jnp.where(i1 == e, s_rows[e], 0.0) for e in range(N_EXPERTS))
    w2 = sum(jnp.where(i2 == e, s_rows[e], 0.0) for e in range(N_EXPERTS))
    inv = 1.0 / (w1 + w2)
    w1, w2 = w1 * inv, w2 * inv
    swap = i2 < i1
    lo = jnp.where(swap, i2, i1) - gsel * EXPERTS_PER_GROUP
    hi = jnp.where(swap, i1, i2) - gsel * EXPERTS_PER_GROUP
    g_lo = jnp.where(swap, w2, w1)
    g_hi = jnp.where(swap, w1, w2)
    pair = jnp.where(lo == 0, hi - 1, jnp.where(lo == 1, hi + 1, PAIRS_PER_GROUP - 1))
    return gsel * PAIRS_PER_GROUP + pair, g_lo, g_hi


def _route_kernel(lat_ref, ctx_ref, oa_ref, ob_ref, wo_ref, mod_ref, modc_ref, g_ref, wr_ref, br_ref, tri_ref,
                  xo_ref, slab_ref, info_ref, cnt_ref, base_ref, *, ctx_t0):
    n_sub, tm = lat_ref.shape[0], lat_ref.shape[1]

    @pl.when((pl.program_id(0) == 0) & (pl.program_id(1) == 0))
    def _():
        base_ref[...] = jnp.zeros_like(base_ref)

    is_ctx = pl.program_id(1) >= ctx_t0
    row128 = lax.broadcasted_iota(jnp.int32, (LANES, tm), 0)
    rowb = lax.broadcasted_iota(jnp.int32, (BUCKET_ROWS, tm), 0)
    row8 = lax.broadcasted_iota(jnp.int32, (8, tm), 0)
    slab_ref[...] = jnp.zeros_like(slab_ref)
    base = base_ref[...]
    for j in range(n_sub):
        mod = jnp.where(is_ctx, modc_ref[...], mod_ref[j])
        mix = jnp.dot(oa_ref[j], wo_ref[0:D_QA, :], preferred_element_type=F32)
        mix += jnp.dot(ob_ref[j], wo_ref[D_QA:2 * D_QA, :], preferred_element_type=F32)
        x = jnp.where(is_ctx, ctx_ref[j], lat_ref[j]) + mod[2:3, :] * mix
        xo_ref[j] = x
        h = _rms(x) * g_ref[...] * (1.0 + mod[4:5, :]) + mod[3:4, :]
        h_hi, h_lo = _split_bf16(h)
        lt = _nt_dot(wr_ref[...], h_hi)
        lt2 = _nt_dot(wr_ref[0:N_EXPERTS, :], h_lo)
        logits_t = lt[0:N_EXPERTS] + lt[N_EXPERTS:2 * N_EXPERTS] + lt2
        bucket, g_lo, g_hi = _route_t(logits_t, br_ref[...])

        gate_t = jnp.where(row128 == 0, g_lo, jnp.where(row128 == 1, g_hi, 0.0))
        for s in range(OUT_SLAB):
            slab_ref[j, pl.ds(s, tm, stride=IN_SLAB), :] = h[:, s * LANES:(s + 1) * LANES]
        slab_ref[j, pl.ds(OUT_SLAB, tm, stride=IN_SLAB), :] = gate_t.T

        member = rowb == bucket
        ranks = jnp.dot(member.astype(BF16), tri_ref[...], preferred_element_type=F32) + base[:, 0:1]
        rank = jnp.sum(jnp.where(member, ranks, 0.0), axis=0, keepdims=True)
        info_ref[j] = jnp.where(row8 == 0, bucket.astype(F32), jnp.where(row8 == 1, rank, 0.0))
        base = base + jnp.sum(member.astype(F32), axis=1, keepdims=True)
    base_ref[...] = base
    cnt_ref[...] = base


def _route(lat_src, ctx_src, ctx_off, alias_lat, oa, ob, w_out, mod, gnorm, wr_t, br_col, n_lat, n_rows, s_tot):
    b = lat_src.shape[0]
    tm = ROUTE_TM
    g = ROUTE_SAMPLES
    n_t = n_rows // tm
    ctx_t0 = n_lat // tm
    tri = (jnp.arange(tm)[:, None] < jnp.arange(tm)[None, :]).astype(BF16)
    const = lambda shape: pl.BlockSpec(shape, lambda bi, t: (0,) * len(shape))
    tile = lambda width: pl.BlockSpec((g, tm, width), lambda bi, t: (bi, t, 0))
    xs, slabs, info, counts = pl.pallas_call(
        functools.partial(_route_kernel, ctx_t0=ctx_t0),
        grid=(b // g, n_t),
        in_specs=[
            *_token_specs(tm, ctx_t0, ctx_off, g),
            tile(D_QA), tile(D_QA),
            const((D_MODEL, D_MODEL)),
            pl.BlockSpec((g, 6, D_MODEL), lambda bi, t: (bi, 0, 0)),
            pl.BlockSpec((None, 6, D_MODEL), lambda bi, t: (b, 0, 0)),
            const((1, D_MODEL)),
            const((2 * N_EXPERTS, D_MODEL)),
            const((N_EXPERTS, 1)),
            const((tm, tm)),
        ],
        out_specs=[
            tile(D_MODEL),
            pl.BlockSpec((g, tm * IN_SLAB, LANES), lambda bi, t: (bi, t, 0)),
            pl.BlockSpec((g, 8, tm), lambda bi, t: (bi, 0, t)),
            const((BUCKET_ROWS, LANES)),
        ],
        out_shape=[
            jax.ShapeDtypeStruct((b, s_tot, D_MODEL), F32),
            jax.ShapeDtypeStruct((b, n_t * tm * IN_SLAB, LANES), F32),
            jax.ShapeDtypeStruct((b, 8, n_t * tm), F32),
            jax.ShapeDtypeStruct((BUCKET_ROWS, LANES), F32),
        ],
        scratch_shapes=[pltpu.VMEM((BUCKET_ROWS, LANES), F32)],
        input_output_aliases={0: 0} if alias_lat else {},
        name="mix_route",
        compiler_params=_cparams(("arbitrary", "arbitrary")),
    )(lat_src, ctx_src, oa, ob, w_out, mod, mod, gnorm, wr_t, br_col, tri)
    return xs, slabs.reshape(b * n_t * tm * IN_SLAB, LANES), info, counts


def _row_copy(src, dst, sem):
    return pltpu.make_async_copy(src, dst, sem)


def _slab(ref, i, n):
    return ref.at[pl.ds(pl.multiple_of(i * n, n), n), :]


def _slab_rows(ref, s, tm, n):
    return ref[pl.ds(s, tm, stride=n), :]


def _permute_kernel(ends_ref, pos_ref, slab_ref, xs_hbm, zero_ref, sem, zsem):
    tm = slab_ref.shape[0] // IN_SLAB
    tile = MOE_TMG * IN_SLAB

    @pl.when((pl.program_id(0) == 0) & (pl.program_id(1) == 0))
    def _():
        zero_ref[...] = jnp.zeros_like(zero_ref)

        def zero_tile(first_token):
            dst = xs_hbm.at[pl.ds(pl.multiple_of(first_token * IN_SLAB, tile), tile), :]
            cp = _row_copy(zero_ref, dst, zsem)
            cp.start()
            cp.wait()

        def pad_fill(g, c):
            prev = ends_ref[jnp.maximum(g - 1, 0)]
            size = ends_ref[g] - jnp.where(g == 0, 0, prev)

            @pl.when(size > 0)
            def _():
                zero_tile(ends_ref[g] - MOE_TMG)
            return c

        lax.fori_loop(0, N_BUCKETS, pad_fill, 0)

        n_tiles = xs_hbm.shape[0] // tile

        def tail_fill(t0, c):
            @pl.when(t0 * MOE_TMG >= ends_ref[N_BUCKETS - 1])
            def _():
                zero_tile(t0 * MOE_TMG)
            return c

        lax.fori_loop(n_tiles - N_BUCKETS, n_tiles, tail_fill, 0)

    def body(r, c):
        _row_copy(_slab(slab_ref, r, IN_SLAB), _slab(xs_hbm, pos_ref[0, 0, r], IN_SLAB), sem).start()
        return c

    lax.fori_loop(0, tm, body, 0, unroll=8)
    _row_copy(slab_ref, xs_hbm.at[pl.ds(0, tm * IN_SLAB), :], sem).wait()


def _permute(slabs, pos, ends, cap, tm):
    n_t = pos.shape[0]
    return pl.pallas_call(
        _permute_kernel,
        grid_spec=pltpu.PrefetchScalarGridSpec(
            num_scalar_prefetch=1,
            grid=(1, n_t),
            in_specs=[
                pl.BlockSpec((1, 1, tm), lambda bi, t, e: (t, 0, 0), memory_space=pltpu.SMEM),
                pl.BlockSpec((tm * IN_SLAB, LANES), lambda bi, t, e: (t, 0)),
            ],
            out_specs=pl.BlockSpec(memory_space=pl.ANY),
            scratch_shapes=[
                pltpu.VMEM((MOE_TMG * IN_SLAB, LANES), F32),
                pltpu.SemaphoreType.DMA(()),
                pltpu.SemaphoreType.DMA(()),
            ],
        ),
        out_shape=jax.ShapeDtypeStruct((cap * IN_SLAB, LANES), F32),
        name="moe_permute",
        compiler_params=_cparams(("arbitrary", "arbitrary")),
    )(ends, pos, slabs)


def _ffn_kernel(ea_ref, eb_ref, val_ref, x_ref, w1a_ref, w3a_ref, w2a_ref, w1b_ref, w3b_ref, w2b_ref, o_ref):
    i = pl.program_id(0)
    tm = x_ref.shape[0] // IN_SLAB

    @pl.when(val_ref[i] > 0)
    def _():
        x = jnp.concatenate([_slab_rows(x_ref, s, tm, IN_SLAB).astype(BF16) for s in range(OUT_SLAB)],
                            axis=1)
        gates = _slab_rows(x_ref, OUT_SLAB, tm, IN_SLAB)
        acc = None
        for k, (w1_ref, w3_ref, w2_ref) in enumerate(((w1a_ref, w3a_ref, w2a_ref), (w1b_ref, w3b_ref, w2b_ref))):
            a = jnp.dot(x, w1_ref[...], preferred_element_type=F32)
            b = jnp.dot(x, w3_ref[...], preferred_element_type=F32)
            u = (a * (1.0 / (1.0 + jnp.exp(-a)))) * b * gates[:, k:k + 1]
            y = jnp.dot(u.astype(BF16), w2_ref[...], preferred_element_type=F32)
            acc = y if acc is None else acc + y
        for s in range(OUT_SLAB):
            o_ref[pl.ds(s, tm, stride=OUT_SLAB), :] = acc[:, s * LANES:(s + 1) * LANES]

    @pl.when(val_ref[i] == 0)
    def _():
        o_ref[...] = jnp.zeros_like(o_ref)


def _ffn(xs_sorted, tile_ea, tile_eb, tile_valid, w1, w3, w2, layer):
    n_tiles = xs_sorted.shape[0] // (MOE_TMG * IN_SLAB)
    up = lambda sel: pl.BlockSpec((None, None, D_MODEL, D_EXPERT),
                                  lambda i, ea, eb, v: (layer, sel(ea, eb)[i], 0, 0))
    down = lambda sel: pl.BlockSpec((None, None, D_EXPERT, D_MODEL),
                                    lambda i, ea, eb, v: (layer, sel(ea, eb)[i], 0, 0))
    first = lambda ea, eb: ea
    second = lambda ea, eb: eb
    return pl.pallas_call(
        _ffn_kernel,
        grid_spec=pltpu.PrefetchScalarGridSpec(
            num_scalar_prefetch=3,
            grid=(n_tiles,),
            in_specs=[
                pl.BlockSpec((MOE_TMG * IN_SLAB, LANES), lambda i, ea, eb, v: (i, 0)),
                up(first), up(first), down(first),
                up(second), up(second), down(second),
            ],
            out_specs=pl.BlockSpec((MOE_TMG * OUT_SLAB, LANES), lambda i, ea, eb, v: (i, 0)),
        ),
        out_shape=jax.ShapeDtypeStruct((n_tiles * MOE_TMG * OUT_SLAB, LANES), F32),
        name="moe_ffn",
        compiler_params=_cparams(("arbitrary",)),
    )(tile_ea, tile_eb, tile_valid, xs_sorted, w1, w3, w2, w1, w3, w2)


def _combine_kernel(pos_ref, x_ref, mod_ref, modc_ref, gfin_ref, ys_hbm, o_ref, buf_ref, sem_a, sem_b,
                    *, n_lat, final):
    tm = x_ref.shape[0]
    half = tm // 2
    halves = ((0, sem_a), (half, sem_b))

    for lo, sem in halves:
        def body(r, c, lo=lo, sem=sem):
            _row_copy(_slab(ys_hbm, pos_ref[0, 0, lo + r], OUT_SLAB), _slab(buf_ref, lo + r, OUT_SLAB), sem).start()
            return c

        lax.fori_loop(0, half, body, 0, unroll=8)

    for lo, sem in halves:
        part = buf_ref.at[pl.ds(lo * OUT_SLAB, half * OUT_SLAB), :]
        _row_copy(ys_hbm.at[pl.ds(0, half * OUT_SLAB), :], part, sem).wait()
        y = jnp.concatenate([buf_ref[pl.ds(lo * OUT_SLAB + s, half, stride=OUT_SLAB), :] for s in range(OUT_SLAB)],
                            axis=1)
        row = pl.program_id(1) * tm + lo + lax.broadcasted_iota(jnp.int32, (half, 1), 0)
        gate = jnp.where(row >= n_lat, modc_ref[5:6, :], mod_ref[5:6, :])
        out = x_ref[lo:lo + half, :] + gate * y
        if final:
            out = _rms(out) * gfin_ref[...]
        o_ref[lo:lo + half, :] = out


def _combine(xs, ys, pos, mod, gfin, n_lat, n_rows, tm, final):
    b, s_tot, _ = xs.shape
    n_t = n_rows // tm
    x_spec = pl.BlockSpec((None, tm, D_MODEL), lambda bi, t: (bi, t, 0))
    return pl.pallas_call(
        functools.partial(_combine_kernel, n_lat=n_lat, final=final),
        grid=(b, n_t),
        in_specs=[
            pl.BlockSpec((1, 1, tm), lambda bi, t: (bi * n_t + t, 0, 0), memory_space=pltpu.SMEM),
            x_spec,
            pl.BlockSpec((None, 6, D_MODEL), lambda bi, t: (bi, 0, 0)),
            pl.BlockSpec((None, 6, D_MODEL), lambda bi, t: (b, 0, 0)),
            pl.BlockSpec((1, D_MODEL), lambda bi, t: (0, 0)),
            pl.BlockSpec(memory_space=pl.ANY),
        ],
        out_specs=x_spec,
        out_shape=jax.ShapeDtypeStruct((b, n_rows, D_MODEL) if final else xs.shape, F32),
        scratch_shapes=[pltpu.VMEM((tm * OUT_SLAB, LANES), F32), pltpu.SemaphoreType.DMA(()),
                        pltpu.SemaphoreType.DMA(())],
        input_output_aliases={} if final else {1: 0},
        name="moe_combine",
        compiler_params=_cparams(("arbitrary", "arbitrary")),
    )(pos, xs, mod, mod, gfin, ys)


def _moe(xs, rows, info, counts, mod, w1, w3, w2, gfin, *, layer, n_lat, n_rows, final):
    b = xs.shape[0]
    n_tok = b * n_rows

    tmg = MOE_TMG
    cap = (n_tok // tmg + N_BUCKETS) * tmg
    cnt = counts[:N_BUCKETS, 0].astype(jnp.int32)
    padded = (cnt + tmg - 1) // tmg * tmg
    ends = jnp.cumsum(padded)
    starts = ends - padded
    bucket = info[:, 0, :].astype(jnp.int32)
    start_of = sum(jnp.where(bucket == k, starts[k], 0) for k in range(N_BUCKETS))
    pos = start_of + info[:, 1, :].astype(jnp.int32)
    tm_p = next(t for t in PERMUTE_TMS if n_tok % t == 0)
    tm_c = n_rows // COMBINE_TILES
    tile_start = jnp.arange(cap // tmg, dtype=jnp.int32) * tmg
    tile_valid = (tile_start < ends[-1]).astype(jnp.int32)
    tile_bkt = sum((tile_start >= ends[k]).astype(jnp.int32) for k in range(N_BUCKETS - 1))
    last_bkt = jnp.max(jnp.where(tile_valid > 0, tile_bkt, 0))
    tile_bkt = jnp.where(tile_valid > 0, tile_bkt, last_bkt)
    pair_lo = (0, 0, 0, 1, 1, 2)
    pair_hi = (1, 2, 3, 2, 3, 3)
    grp, pair = tile_bkt // PAIRS_PER_GROUP, tile_bkt % PAIRS_PER_GROUP
    tile_ea = grp * EXPERTS_PER_GROUP + sum(jnp.where(pair == p, pair_lo[p], 0) for p in range(PAIRS_PER_GROUP))
    tile_eb = grp * EXPERTS_PER_GROUP + sum(jnp.where(pair == p, pair_hi[p], 0) for p in range(PAIRS_PER_GROUP))

    xs_sorted = _permute(rows, pos.reshape(n_tok // tm_p, 1, tm_p), ends.astype(jnp.int32), cap, tm_p)
    ys = _ffn(xs_sorted, tile_ea.astype(jnp.int32), tile_eb.astype(jnp.int32), tile_valid, w1, w3, w2, layer)
    return _combine(xs, ys, pos.reshape(n_tok // tm_c, 1, tm_c), mod, gfin, n_lat, n_rows, tm_c, final)


def _rope_tables(n_lat, n_ctx):
    pos = jnp.arange(n_lat)
    row = (pos // GRID_W).astype(F32)
    col = (pos % GRID_W).astype(F32)
    n_freq = HEAD_DIM // 4
    inv = 1.0 / (ROPE_BASE ** (jnp.arange(n_freq, dtype=F32) / n_freq))
    lane = jnp.arange(LANES) % HEAD_DIM
    axis = lane // 32
    second = (lane % 32) // 16
    freq = inv[lane % 16]
    ang = jnp.where(axis[None, :] == 0, row[:, None], col[:, None]) * freq[None, :]
    cos = jnp.cos(ang)
    sin = jnp.sin(ang) * jnp.where(second == 0, -1.0, 1.0)[None, :]
    cos = jnp.concatenate([cos, jnp.ones((n_ctx, LANES), F32)], axis=0)
    sin = jnp.concatenate([sin, jnp.zeros((n_ctx, LANES), F32)], axis=0)
    q_scale = HEAD_DIM ** -0.5 * LOG2E
    return cos * q_scale, sin * q_scale, cos, sin


def _proj_cols():
    base = jnp.arange(D_IN)
    kb0 = base[2048:2112]
    kb1 = base[2112:2176]
    return jnp.concatenate([base[:2048], kb0, kb0, kb1, kb1, base[2176:]])


def kernel(x, c, ctx, c_ctx, w_ada, b_ada, norm_attn, norm_ffn, w_in, w_out, lambda_qk, subln, sink, w_router,
           b_router, w1, w3, w2, norm_final):
    b, n_lat, _ = x.shape
    n_ctx = ctx.shape[1]
    depth = w_in.shape[0]

    s_tot = n_lat + n_ctx
    c_all = jnp.zeros((MOD_ROWS, D_MODEL), F32).at[:b].set(c).at[b].set(c_ctx)
    mod_all = _ada(c_all, w_ada, b_ada).reshape(depth, MOD_ROWS, 6, D_MODEL)
    tables = _rope_tables(n_lat, n_ctx)
    cols = _proj_cols()

    wr_hi, wr_lo = _split_bf16(w_router.T)
    wr_t = jnp.concatenate([wr_hi, wr_lo], axis=0)
    br_col = b_router.reshape(N_EXPERTS, 1)
    gfin = norm_final.reshape(1, D_MODEL)

    w1b, w3b, w2b = w1.astype(BF16), w3.astype(BF16), w2.astype(BF16)
    ctx_t0 = n_lat // PROJ_TM

    lat_src, ctx_src, ctx_off = x, ctx, 0
    out = None
    for l in range(depth):
        last = l == depth - 1
        lam_init = 0.8 - 0.6 * math.exp(-0.3 * l)
        mod = mod_all[l]
        w_in_l = w_in[l][:, cols].astype(BF16)
        proj = _proj(lat_src, ctx_src, ctx_off, mod, norm_attn[l].reshape(1, D_MODEL), w_in_l, tables, n_lat, s_tot)
        oa = _attn_a(proj, lambda_qk[l], subln[l].reshape(2 * HEAD_DIM, 1), n_lat, not last, lam_init)
        ob = _attn_b(proj, sink[l], n_lat, not last)
        n_rows = n_lat if last else s_tot
        in_place = l > 0
        xs, rows, info, counts = _route(lat_src, ctx if in_place else ctx_src, 0 if in_place else ctx_off, in_place,
                                        oa, ob, w_out[l].astype(BF16), mod, norm_ffn[l].reshape(1, D_MODEL),
                                        wr_t, br_col, n_lat, n_rows, s_tot)
        res = _moe(xs, rows, info, counts, mod, w1b, w3b, w2b, gfin, layer=l, n_lat=n_lat, n_rows=n_rows,
                   final=last)
        if last:
            out = res
        else:
            lat_src, ctx_src, ctx_off = res, res, ctx_t0
    return out
```

```python
import functools
import math

import jax
import jax.numpy as jnp
from jax import lax
from jax.experimental import pallas as pl
from jax.experimental.pallas import tpu as pltpu

F32 = jnp.float32
BF16 = jnp.bfloat16

D_MODEL = 1024
HEAD_DIM = 64
HA = 4
HB = 8
KVB = 2
GB = HB // KVB
GRID_W = 64
WINDOW = 128
BLOCK = 128
ROPE_BASE = 10000.0
N_EXPERTS = 16
N_GROUPS = 4
EXPERTS_PER_GROUP = N_EXPERTS // N_GROUPS
D_EXPERT = 512
EPS = 1e-6
NEG = -1e30
LOG2E = 1.4426950408889634

LANES = 128
D_QA = HA * 2 * HEAD_DIM
D_IN = 2304
N_PROJ_TILES = 19
D_PROJ = N_PROJ_TILES * LANES
Q_TILES = (0, 1, 2, 3, 12, 13, 14, 15)
K_TILES = (4, 5, 6, 7, 16, 17)

VMEM_LIMIT = 56 * 1024 * 1024

PROJ_TM = 256
ATTN_TQ = 1024
ATTN_KC = 512
ROUTE_TM = 256
ROUTE_SAMPLES = 2
PAIRS_PER_GROUP = 6
N_BUCKETS = N_GROUPS * PAIRS_PER_GROUP
BUCKET_ROWS = 32
MOE_TMG = 256
PERMUTE_TMS = (2048, 1024, 512, 256)
DMA_UNROLL = 8
COMBINE_TILES = 4
OUT_SLAB = D_MODEL // LANES
IN_SLAB = 2 * OUT_SLAB
MOD_ROWS = 16


def _cparams(sem):
    return pltpu.CompilerParams(dimension_semantics=sem, vmem_limit_bytes=VMEM_LIMIT)


def _nt_dot(a, b):
    return lax.dot_general(a, b, (((1,), (1,)), ((), ())), preferred_element_type=F32)


def _split_bf16(x):
    hi = x.astype(BF16)
    lo = (x - hi.astype(F32)).astype(BF16)
    return hi, lo


def _rms(x):
    return x * lax.rsqrt(jnp.mean(x * x, axis=-1, keepdims=True) + EPS)


def _ada_kernel(c_ref, w_ref, b_ref, o_ref):
    c = c_ref[...]
    a = c * (1.0 / (1.0 + jnp.exp(-c)))
    a_hi, a_lo = _split_bf16(a)
    w_hi, w_lo = _split_bf16(w_ref[...])
    acc = jnp.dot(a_hi, w_hi, preferred_element_type=F32)
    acc += jnp.dot(a_hi, w_lo, preferred_element_type=F32)
    acc += jnp.dot(a_lo, w_hi, preferred_element_type=F32)
    o_ref[...] = acc + b_ref[...]


def _ada(c_all, w_ada, b_ada):
    depth = w_ada.shape[0]
    n_out = w_ada.shape[2]
    tn = D_MODEL
    return pl.pallas_call(
        _ada_kernel,
        grid=(depth, n_out // tn),
        in_specs=[
            pl.BlockSpec((MOD_ROWS, D_MODEL), lambda l, n: (0, 0)),
            pl.BlockSpec((None, D_MODEL, tn), lambda l, n: (l, 0, n)),
            pl.BlockSpec((None, 1, tn), lambda l, n: (l, 0, n)),
        ],
        out_specs=pl.BlockSpec((None, MOD_ROWS, tn), lambda l, n: (l, 0, n)),
        out_shape=jax.ShapeDtypeStruct((depth, MOD_ROWS, n_out), F32),
        name="ada_mod",
        compiler_params=_cparams(("parallel", "parallel")),
    )(c_all, w_ada, b_ada.reshape(depth, 1, n_out))


def _token_specs(tm, ctx_t0, ctx_off, samples=None):
    lat = pl.BlockSpec((samples, tm, D_MODEL), lambda bi, t, *_: (bi, jnp.minimum(t, ctx_t0 - 1), 0))
    ctx = pl.BlockSpec((samples, tm, D_MODEL), lambda bi, t, *_: (bi, ctx_off + jnp.maximum(t - ctx_t0, 0), 0))
    return lat, ctx


def _proj_kernel(lat_ref, ctx_ref, mod_ref, g_ref, w_ref, cq_ref, sq_ref, ck_ref, sk_ref, o_ref, *, ctx_t0):
    x = jnp.where(pl.program_id(1) >= ctx_t0, ctx_ref[...], lat_ref[...])
    h = _rms(x) * g_ref[...] * (1.0 + mod_ref[1:2, :]) + mod_ref[0:1, :]
    acc = jnp.dot(h.astype(BF16), w_ref[...], preferred_element_type=F32)
    tm = x.shape[0]
    lane = lax.broadcasted_iota(jnp.int32, (tm, LANES), 1)
    first_half = (lane % 32) < 16
    for t in range(N_PROJ_TILES):
        a = acc[:, t * LANES:(t + 1) * LANES]
        if t in Q_TILES or t in K_TILES:
            cos, sin = (cq_ref, sq_ref) if t in Q_TILES else (ck_ref, sk_ref)
            partner = jnp.where(first_half, pltpu.roll(a, LANES - 16, 1), pltpu.roll(a, 16, 1))
            a = a * cos[...] + partner * sin[...]
        o_ref[:, t * LANES:(t + 1) * LANES] = a.astype(BF16)


def _proj(lat_src, ctx_src, ctx_off, mod, gnorm, w, tables, n_lat, s_tot):
    b = lat_src.shape[0]
    tm = PROJ_TM
    n_t = s_tot // tm
    ctx_t0 = n_lat // tm
    tab_spec = pl.BlockSpec((tm, LANES), lambda bi, t: (t, 0))
    return pl.pallas_call(
        functools.partial(_proj_kernel, ctx_t0=ctx_t0),
        grid=(b, n_t),
        in_specs=[
            *_token_specs(tm, ctx_t0, ctx_off),
            pl.BlockSpec((None, 6, D_MODEL), lambda bi, t: (jnp.where(t >= ctx_t0, b, bi), 0, 0)),
            pl.BlockSpec((1, D_MODEL), lambda bi, t: (0, 0)),
            pl.BlockSpec((D_MODEL, D_PROJ), lambda bi, t: (0, 0)),
            tab_spec, tab_spec, tab_spec, tab_spec,
        ],
        out_specs=pl.BlockSpec((None, tm, D_PROJ), lambda bi, t: (bi, t, 0)),
        out_shape=jax.ShapeDtypeStruct((b, s_tot, D_PROJ), BF16),
        name="in_proj",
        compiler_params=_cparams(("parallel", "parallel")),
    )(lat_src, ctx_src, mod, gnorm, w, *tables)


def _transpose_into(vt_ref, v_ref, s_tot):
    for c in range(s_tot // LANES):
        blk = v_ref[c * LANES:(c + 1) * LANES, :].astype(F32)
        vt_ref[:, c * LANES:(c + 1) * LANES] = blk.T.astype(BF16)


def _attn_a_kernel(q_ref, k_ref, v_ref, lam_ref, subln_ref, o_ref, vt_ref, acc1_ref, acc2_ref,
                   s0_ref, s1_ref, *, n_lat, n_ctx, do_ctx, lam_init):
    s_tot = n_lat + n_ctx
    _transpose_into(vt_ref, v_ref, s_tot)

    lq = lam_ref[...]
    lam = (jnp.exp(jnp.sum(lq[0:1] * lq[1:2], axis=1, keepdims=True))
           - jnp.exp(jnp.sum(lq[2:3] * lq[3:4], axis=1, keepdims=True)) + lam_init)
    lane = lax.broadcasted_iota(jnp.int32, (1, LANES), 1)
    m_lo = (lane < HEAD_DIM).astype(BF16)
    m_hi = (lane >= HEAD_DIM).astype(BF16)
    subln = subln_ref[...]

    def finish(o1, l1, o2, l2):
        o = o1 * (1.0 / l1) - lam * (o2 * (1.0 / l2))
        o = o * lax.rsqrt(jnp.mean(o * o, axis=0, keepdims=True) + EPS)
        o = o * subln * (1.0 - lam_init)
        return o.T.astype(BF16)

    def one_shot(ks, vt, qm):
        s = _nt_dot(ks, qm)
        m = jnp.max(s, axis=0, keepdims=True)
        p = jnp.exp2(s - m)
        l = jnp.sum(p, axis=0, keepdims=True)
        return jnp.dot(vt, p.astype(BF16), preferred_element_type=F32), m, l

    tq, kc = ATTN_TQ, ATTN_KC
    n_kv = n_lat // kc

    def q_tile(i, carry):
        q0 = pl.multiple_of(i * tq, tq)
        qs = q_ref[pl.ds(q0, tq), :]
        qm = (qs * m_lo, qs * m_hi)

        def score(k0, size, dst_ref):
            ks = k_ref[pl.ds(k0, size), :]
            maxes = []
            for mp in range(2):
                s = _nt_dot(ks, qm[mp])
                dst_ref[mp, 0:size, :] = s
                maxes.append(jnp.max(s, axis=0, keepdims=True))
            return tuple(maxes)

        def update(src_ref, smaxes, k0, size, c):
            vt = vt_ref[:, pl.ds(k0, size)]
            out = []
            for mp, acc_ref in ((0, acc1_ref), (1, acc2_ref)):
                s = src_ref[mp, 0:size, :]
                smax = smaxes[mp]
                if c is None:
                    m_new = smax
                    p = jnp.exp2(s - m_new)
                    l_new = jnp.sum(p, axis=0, keepdims=True)
                    acc_ref[...] = jnp.dot(vt, p.astype(BF16), preferred_element_type=F32)
                else:
                    m, l = c[2 * mp], c[2 * mp + 1]
                    m_new = jnp.maximum(m, smax)
                    alpha = jnp.exp2(m - m_new)
                    p = jnp.exp2(s - m_new)
                    l_new = alpha * l + jnp.sum(p, axis=0, keepdims=True)
                    acc_ref[...] = alpha * acc_ref[...] + jnp.dot(vt, p.astype(BF16),
                                                                  preferred_element_type=F32)
                out += [m_new, l_new]
            return tuple(out)

        mx0 = score(0, kc, s0_ref)
        mx1 = score(kc, kc, s1_ref)
        c = update(s0_ref, mx0, 0, kc, None)

        def pair(t, st):
            c, mx1 = st[:4], st[4:]
            ka = pl.multiple_of((2 * t + 1) * kc, kc)
            kb = pl.multiple_of(ka + kc, kc)
            mx0 = score(kb, kc, s0_ref)
            c = update(s1_ref, mx1, ka, kc, c)
            mx1 = score(pl.multiple_of(kb + kc, kc), kc, s1_ref)
            return update(s0_ref, mx0, kb, kc, c) + mx1

        st = lax.fori_loop(0, n_kv // 2 - 2, pair, c + mx1)
        c, mx1 = st[:4], st[4:]
        last = (n_kv - 1) * kc
        mx0 = score(last - kc, kc, s0_ref)
        c = update(s1_ref, mx1, last - 2 * kc, kc, c)
        mx1 = score(last, kc + n_ctx, s1_ref)
        c = update(s0_ref, mx0, last - kc, kc, c)
        m1, l1, m2, l2 = update(s1_ref, mx1, last, kc + n_ctx, c)
        o_ref[pl.ds(q0, tq), :] = finish(acc1_ref[...], l1, acc2_ref[...], l2)
        return carry

    lax.fori_loop(0, n_lat // tq, q_tile, 0)

    if do_ctx:
        qs = q_ref[n_lat:s_tot, :]
        kctx = k_ref[n_lat:s_tot, :]
        vctx = vt_ref[:, n_lat:s_tot]
        o1, _, l1 = one_shot(kctx, vctx, qs * m_lo)
        o2, _, l2 = one_shot(kctx, vctx, qs * m_hi)
        o_ref[n_lat:s_tot, :] = finish(o1, l1, o2, l2)
    else:
        o_ref[n_lat:s_tot, :] = jnp.zeros((n_ctx, LANES), BF16)


def _attn_a(proj, lam_qk, subln_col, n_lat, do_ctx, lam_init):
    b, s_tot, _ = proj.shape
    n_ctx = s_tot - n_lat
    n_kv = n_lat // ATTN_KC
    assert n_lat % ATTN_TQ == 0 and n_lat % ATTN_KC == 0 and n_kv >= 4 and n_kv % 2 == 0, n_lat
    kern = functools.partial(_attn_a_kernel, n_lat=n_lat, n_ctx=n_ctx, do_ctx=do_ctx, lam_init=lam_init)
    blk = lambda off: pl.BlockSpec((None, s_tot, LANES), lambda bi, h: (bi, 0, off + h))
    return pl.pallas_call(
        kern,
        grid=(b, HA),
        in_specs=[
            blk(0), blk(4), blk(8),
            pl.BlockSpec((4, HEAD_DIM), lambda bi, h: (0, 0)),
            pl.BlockSpec((2 * HEAD_DIM, 1), lambda bi, h: (0, 0)),
        ],
        out_specs=pl.BlockSpec((None, s_tot, LANES), lambda bi, h: (bi, 0, h)),
        out_shape=jax.ShapeDtypeStruct((b, s_tot, HA * LANES), BF16),
        scratch_shapes=[
            pltpu.VMEM((LANES, s_tot), BF16),
            pltpu.VMEM((LANES, ATTN_TQ), F32),
            pltpu.VMEM((LANES, ATTN_TQ), F32),
            pltpu.VMEM((2, ATTN_KC, ATTN_TQ), F32),
            pltpu.VMEM((2, ATTN_KC + n_ctx, ATTN_TQ), F32),
        ],
        name="attn_diff",
        compiler_params=_cparams(("parallel", "parallel")),
    )(proj, proj, proj, lam_qk, subln_col)


def _attn_b_kernel(sink_ref, q_ref, k_ref, v_ref, o_ref, vt_ref, sc0_ref, sw0_ref, sc1_ref, sw1_ref,
                   *, n_lat, n_ctx, do_ctx):
    s_tot = n_lat + n_ctx
    j = pl.program_id(1)
    _transpose_into(vt_ref, v_ref, s_tot)

    lane = lax.broadcasted_iota(jnp.int32, (1, LANES), 1)
    m_lo = (lane < HEAD_DIM).astype(BF16)
    m_hi = (lane >= HEAD_DIM).astype(BF16)
    nq = BLOCK
    win = 3 * BLOCK
    sink_row = jnp.concatenate(
        [jnp.full((1, nq), sink_ref[j * GB + g] * LOG2E, F32) for g in range(GB)], axis=1)
    rc = (lax.broadcasted_iota(jnp.int32, (win, nq), 0) - lax.broadcasted_iota(jnp.int32, (win, nq), 1))
    v0 = pl.multiple_of(j * HEAD_DIM, HEAD_DIM)

    def q_stack(q0):
        qa = q_ref[pl.ds(q0, nq), 0:LANES]
        qb = q_ref[pl.ds(q0, nq), LANES:2 * LANES]
        return jnp.concatenate([qa * m_lo, qa * m_hi, qb * m_lo, qb * m_hi], axis=0)

    def emit(q0, o):
        for pr in range(2):
            pair = jnp.concatenate([o[:, (2 * pr) * nq:(2 * pr + 1) * nq],
                                    o[:, (2 * pr + 1) * nq:(2 * pr + 2) * nq]], axis=0)
            o_ref[pl.ds(q0, nq), pr * LANES:(pr + 1) * LANES] = pair.T.astype(BF16)

    kctx = k_ref[n_lat:s_tot, :]
    vctx = vt_ref[pl.ds(v0, HEAD_DIM), n_lat:s_tot]

    def win_start(i):
        return pl.multiple_of(jnp.clip((i - 1) * nq, 0, n_lat - win), nq)

    def score(i, sc_ref, sw_ref):
        qs = q_stack(pl.multiple_of(i * nq, nq))
        start = win_start(i)
        sc_ref[...] = _nt_dot(kctx, qs)
        s_w = _nt_dot(k_ref[pl.ds(start, win), :], qs)
        rel = rc + (start - i * nq)
        ok = (rel <= WINDOW) & (rel >= -WINDOW)
        sw_ref[...] = jnp.where(jnp.concatenate([ok] * GB, axis=1), s_w, NEG)

    def attend(i, sc_ref, sw_ref):
        vw = vt_ref[pl.ds(v0, HEAD_DIM), pl.ds(win_start(i), win)]
        s_c = sc_ref[...]
        s_w = sw_ref[...]
        m = jnp.maximum(jnp.maximum(jnp.max(s_c, axis=0, keepdims=True),
                                    jnp.max(s_w, axis=0, keepdims=True)), sink_row)
        p_c = jnp.exp2(s_c - m)
        p_w = jnp.exp2(s_w - m)
        l = (jnp.sum(p_c, axis=0, keepdims=True) + jnp.sum(p_w, axis=0, keepdims=True)
             + jnp.exp2(sink_row - m))
        o = (jnp.dot(vctx, p_c.astype(BF16), preferred_element_type=F32)
             + jnp.dot(vw, p_w.astype(BF16), preferred_element_type=F32))
        emit(pl.multiple_of(i * nq, nq), o * (1.0 / l))

    n_blk = n_lat // nq
    score(0, sc0_ref, sw0_ref)

    def pair(t, carry):
        score(2 * t + 1, sc1_ref, sw1_ref)
        attend(2 * t, sc0_ref, sw0_ref)
        score(2 * t + 2, sc0_ref, sw0_ref)
        attend(2 * t + 1, sc1_ref, sw1_ref)
        return carry

    lax.fori_loop(0, n_blk // 2 - 1, pair, 0)
    score(n_blk - 1, sc1_ref, sw1_ref)
    attend(n_blk - 2, sc0_ref, sw0_ref)
    attend(n_blk - 1, sc1_ref, sw1_ref)

    for cb in range(n_ctx // nq):
        q0 = n_lat + cb * nq
        if do_ctx:
            qs = q_stack(q0)
            s_c = _nt_dot(kctx, qs)
            m = jnp.maximum(jnp.max(s_c, axis=0, keepdims=True), sink_row)
            p_c = jnp.exp2(s_c - m)
            l = jnp.sum(p_c, axis=0, keepdims=True) + jnp.exp2(sink_row - m)
            o = jnp.dot(vctx, p_c.astype(BF16), preferred_element_type=F32)
            emit(q0, o * (1.0 / l))
        else:
            o_ref[q0:q0 + nq, :] = jnp.zeros((nq, 2 * LANES), BF16)


def _attn_b(proj, sink, n_lat, do_ctx):
    b, s_tot, _ = proj.shape
    n_ctx = s_tot - n_lat
    kern = functools.partial(_attn_b_kernel, n_lat=n_lat, n_ctx=n_ctx, do_ctx=do_ctx)
    return pl.pallas_call(
        kern,
        grid_spec=pltpu.PrefetchScalarGridSpec(
            num_scalar_prefetch=1,
            grid=(b, KVB),
            in_specs=[
                pl.BlockSpec((None, s_tot, 2 * LANES), lambda bi, j, s: (bi, 0, 6 + j)),
                pl.BlockSpec((None, s_tot, LANES), lambda bi, j, s: (bi, 0, 16 + j)),
                pl.BlockSpec((None, s_tot, LANES), lambda bi, j, s: (bi, 0, 18)),
            ],
            out_specs=pl.BlockSpec((None, s_tot, 2 * LANES), lambda bi, j, s: (bi, 0, j)),
            scratch_shapes=[
                pltpu.VMEM((LANES, s_tot), BF16),
                pltpu.VMEM((n_ctx, GB * BLOCK), F32),
                pltpu.VMEM((3 * BLOCK, GB * BLOCK), F32),
                pltpu.VMEM((n_ctx, GB * BLOCK), F32),
                pltpu.VMEM((3 * BLOCK, GB * BLOCK), F32),
            ],
        ),
        out_shape=jax.ShapeDtypeStruct((b, s_tot, HB * HEAD_DIM), BF16),
        name="attn_win",
        compiler_params=_cparams(("parallel", "parallel")),
    )(sink, proj, proj, proj)


def _route_t(logits_t, bias_col):
    s = 1.0 / (1.0 + jnp.exp(-logits_t))
    sb = s + bias_col
    s_rows = [s[e:e + 1, :] for e in range(N_EXPERTS)]
    sb_rows = [sb[e:e + 1, :] for e in range(N_EXPERTS)]
    best = None
    gsel = None
    for g in range(N_GROUPS):
        a, b, c, d = sb_rows[4 * g:4 * g + 4]
        hi1, lo1 = jnp.maximum(a, b), jnp.minimum(a, b)
        hi2, lo2 = jnp.maximum(c, d), jnp.minimum(c, d)
        gs = jnp.maximum(hi1, hi2) + jnp.maximum(jnp.minimum(hi1, hi2), jnp.maximum(lo1, lo2))
        if g == 0:
            best, gsel = gs, jnp.zeros_like(gs, dtype=jnp.int32)
        else:
            better = gs > best
            gsel = jnp.where(better, g, gsel)
            best = jnp.where(better, gs, best)
    masked = [jnp.where(gsel == (e // EXPERTS_PER_GROUP), sb_rows[e], -jnp.inf) for e in range(N_EXPERTS)]

    def argtop(vals):
        v, i = vals[0], jnp.zeros_like(gsel)
        for e in range(1, N_EXPERTS):
            better = vals[e] > v
            i = jnp.where(better, e, i)
            v = jnp.where(better, vals[e], v)
        return i

    i1 = argtop(masked)
    i2 = argtop([jnp.where(i1 == e, -jnp.inf, masked[e]) for e in range(N_EXPERTS)])
    w1 = sum(jnp.where(i1 == e, s_rows[e], 0.0) for e in range(N_EXPERTS))
    w2 = sum(jnp.where(i2 == e, s_rows[e], 0.0) for e in range(N_EXPERTS))
    inv = 1.0 / (w1 + w2)
    w1, w2 = w1 * inv, w2 * inv
    swap = i2 < i1
    lo = jnp.where(swap, i2, i1) - gsel * EXPERTS_PER_GROUP
    hi = jnp.where(swap, i1, i2) - gsel * EXPERTS_PER_GROUP
    g_lo = jnp.where(swap, w2, w1)
    g_hi = jnp.where(swap, w1, w2)
    pair = jnp.where(lo == 0, hi - 1, jnp.where(lo == 1, hi + 1, PAIRS_PER_GROUP - 1))
    return gsel * PAIRS_PER_GROUP + pair, g_lo, g_hi


def _route_kernel(lat_ref, ctx_ref, oa_ref, ob_ref, wo_ref, mod_ref, modc_ref, g_ref, wr_ref, br_ref, tri_ref,
                  xo_ref, slab_ref, info_ref, cnt_ref, base_ref, *, ctx_t0):
    n_sub, tm = lat_ref.shape[0], lat_ref.shape[1]

    @pl.when((pl.program_id(0) == 0) & (pl.program_id(1) == 0))
    def _():
        base_ref[...] = jnp.zeros_like(base_ref)

    is_ctx = pl.program_id(1) >= ctx_t0
    row128 = lax.broadcasted_iota(jnp.int32, (LANES, tm), 0)
    rowb = lax.broadcasted_iota(jnp.int32, (BUCKET_ROWS, tm), 0)
    row8 = lax.broadcasted_iota(jnp.int32, (8, tm), 0)
    slab_ref[...] = jnp.zeros_like(slab_ref)
    base = base_ref[...]
    for j in range(n_sub):
        mod = jnp.where(is_ctx, modc_ref[...], mod_ref[j])
        mix = jnp.dot(oa_ref[j], wo_ref[0:D_QA, :], preferred_element_type=F32)
        mix += jnp.dot(ob_ref[j], wo_ref[D_QA:2 * D_QA, :], preferred_element_type=F32)
        x = jnp.where(is_ctx, ctx_ref[j], lat_ref[j]) + mod[2:3, :] * mix
        xo_ref[j] = x
        h = _rms(x) * g_ref[...] * (1.0 + mod[4:5, :]) + mod[3:4, :]
        h_hi, h_lo = _split_bf16(h)
        lt = _nt_dot(wr_ref[...], h_hi)
        lt2 = _nt_dot(wr_ref[0:N_EXPERTS, :], h_lo)
        logits_t = lt[0:N_EXPERTS] + lt[N_EXPERTS:2 * N_EXPERTS] + lt2
        bucket, g_lo, g_hi = _route_t(logits_t, br_ref[...])

        gate_t = jnp.where(row128 == 0, g_lo, jnp.where(row128 == 1, g_hi, 0.0))
        for s in range(OUT_SLAB):
            slab_ref[j, pl.ds(s, tm, stride=IN_SLAB), :] = h[:, s * LANES:(s + 1) * LANES]
        slab_ref[j, pl.ds(OUT_SLAB, tm, stride=IN_SLAB), :] = gate_t.T

        member = rowb == bucket
        ranks = jnp.dot(member.astype(BF16), tri_ref[...], preferred_element_type=F32) + base[:, 0:1]
        rank = jnp.sum(jnp.where(member, ranks, 0.0), axis=0, keepdims=True)
        info_ref[j] = jnp.where(row8 == 0, bucket.astype(F32), jnp.where(row8 == 1, rank, 0.0))
        base = base + jnp.sum(member.astype(F32), axis=1, keepdims=True)
    base_ref[...] = base
    cnt_ref[...] = base


def _route(lat_src, ctx_src, ctx_off, alias_lat, oa, ob, w_out, mod, gnorm, wr_t, br_col, n_lat, n_rows, s_tot):
    b = lat_src.shape[0]
    tm = ROUTE_TM
    g = ROUTE_SAMPLES
    n_t = n_rows // tm
    ctx_t0 = n_lat // tm
    tri = (jnp.arange(tm)[:, None] < jnp.arange(tm)[None, :]).astype(BF16)
    const = lambda shape: pl.BlockSpec(shape, lambda bi, t: (0,) * len(shape))
    tile = lambda width: pl.BlockSpec((g, tm, width), lambda bi, t: (bi, t, 0))
    xs, slabs, info, counts = pl.pallas_call(
        functools.partial(_route_kernel, ctx_t0=ctx_t0),
        grid=(b // g, n_t),
        in_specs=[
            *_token_specs(tm, ctx_t0, ctx_off, g),
            tile(D_QA), tile(D_QA),
            const((D_MODEL, D_MODEL)),
            pl.BlockSpec((g, 6, D_MODEL), lambda bi, t: (bi, 0, 0)),
            pl.BlockSpec((None, 6, D_MODEL), lambda bi, t: (b, 0, 0)),
            const((1, D_MODEL)),
            const((2 * N_EXPERTS, D_MODEL)),
            const((N_EXPERTS, 1)),
            const((tm, tm)),
        ],
        out_specs=[
            tile(D_MODEL),
            pl.BlockSpec((g, tm * IN_SLAB, LANES), lambda bi, t: (bi, t, 0)),
            pl.BlockSpec((g, 8, tm), lambda bi, t: (bi, 0, t)),
            const((BUCKET_ROWS, LANES)),
        ],
        out_shape=[
            jax.ShapeDtypeStruct((b, s_tot, D_MODEL), F32),
            jax.ShapeDtypeStruct((b, n_t * tm * IN_SLAB, LANES), F32),
            jax.ShapeDtypeStruct((b, 8, n_t * tm), F32),
            jax.ShapeDtypeStruct((BUCKET_ROWS, LANES), F32),
        ],
        scratch_shapes=[pltpu.VMEM((BUCKET_ROWS, LANES), F32)],
        input_output_aliases={0: 0} if alias_lat else {},
        name="mix_route",
        compiler_params=_cparams(("arbitrary", "arbitrary")),
    )(lat_src, ctx_src, oa, ob, w_out, mod, mod, gnorm, wr_t, br_col, tri)
    return xs, slabs.reshape(b * n_t * tm * IN_SLAB, LANES), info, counts


def _row_copy(src, dst, sem):
    return pltpu.make_async_copy(src, dst, sem)


def _slab(ref, i, n):
    return ref.at[pl.ds(pl.multiple_of(i * n, n), n), :]


def _slab_rows(ref, s, tm, n):
    return ref[pl.ds(s, tm, stride=n), :]


def _permute_kernel(ends_ref, pos_ref, slab_ref, xs_hbm, zero_ref, sem, zsem):
    tm = slab_ref.shape[0] // IN_SLAB
    tile = MOE_TMG * IN_SLAB

    @pl.when((pl.program_id(0) == 0) & (pl.program_id(1) == 0))
    def _():
        zero_ref[...] = jnp.zeros_like(zero_ref)

        def zero_tile(first_token):
            dst = xs_hbm.at[pl.ds(pl.multiple_of(first_token * IN_SLAB, tile), tile), :]
            cp = _row_copy(zero_ref, dst, zsem)
            cp.start()
            cp.wait()

        def pad_fill(g, c):
            prev = ends_ref[jnp.maximum(g - 1, 0)]
            size = ends_ref[g] - jnp.where(g == 0, 0, prev)

            @pl.when(size > 0)
            def _():
                zero_tile(ends_ref[g] - MOE_TMG)
            return c

        lax.fori_loop(0, N_BUCKETS, pad_fill, 0)

        n_tiles = xs_hbm.shape[0] // tile

        def tail_fill(t0, c):
            @pl.when(t0 * MOE_TMG >= ends_ref[N_BUCKETS - 1])
            def _():
                zero_tile(t0 * MOE_TMG)
            return c

        lax.fori_loop(n_tiles - N_BUCKETS, n_tiles, tail_fill, 0)

    def body(g, c):
        for u in range(DMA_UNROLL):
            r = g * DMA_UNROLL + u
            _row_copy(_slab(slab_ref, r, IN_SLAB), _slab(xs_hbm, pos_ref[0, 0, r], IN_SLAB), sem).start(priority=u % 2)
        return c

    lax.fori_loop(0, tm // DMA_UNROLL, body, 0)
    _row_copy(slab_ref, xs_hbm.at[pl.ds(0, tm * IN_SLAB), :], sem).wait()


def _permute(slabs, pos, ends, cap, tm):
    n_t = pos.shape[0]
    return pl.pallas_call(
        _permute_kernel,
        grid_spec=pltpu.PrefetchScalarGridSpec(
            num_scalar_prefetch=1,
            grid=(1, n_t),
            in_specs=[
                pl.BlockSpec((1, 1, tm), lambda bi, t, e: (t, 0, 0), memory_space=pltpu.SMEM),
                pl.BlockSpec((tm * IN_SLAB, LANES), lambda bi, t, e: (t, 0)),
            ],
            out_specs=pl.BlockSpec(memory_space=pl.ANY),
            scratch_shapes=[
                pltpu.VMEM((MOE_TMG * IN_SLAB, LANES), F32),
                pltpu.SemaphoreType.DMA(()),
                pltpu.SemaphoreType.DMA(()),
            ],
        ),
        out_shape=jax.ShapeDtypeStruct((cap * IN_SLAB, LANES), F32),
        name="moe_permute",
        compiler_params=_cparams(("arbitrary", "arbitrary")),
    )(ends, pos, slabs)


def _ffn_kernel(ea_ref, eb_ref, val_ref, x_ref, w1a_ref, w3a_ref, w2a_ref, w1b_ref, w3b_ref, w2b_ref, o_ref):
    i = pl.program_id(0)
    tm = x_ref.shape[0] // IN_SLAB

    @pl.when(val_ref[i] > 0)
    def _():
        x = jnp.concatenate([_slab_rows(x_ref, s, tm, IN_SLAB).astype(BF16) for s in range(OUT_SLAB)],
                            axis=1)
        gates = _slab_rows(x_ref, OUT_SLAB, tm, IN_SLAB)
        acc = None
        for k, (w1_ref, w3_ref, w2_ref) in enumerate(((w1a_ref, w3a_ref, w2a_ref), (w1b_ref, w3b_ref, w2b_ref))):
            a = jnp.dot(x, w1_ref[...], preferred_element_type=F32)
            b = jnp.dot(x, w3_ref[...], preferred_element_type=F32)
            u = (a * (1.0 / (1.0 + jnp.exp(-a)))) * b * gates[:, k:k + 1]
            y = jnp.dot(u.astype(BF16), w2_ref[...], preferred_element_type=F32)
            acc = y if acc is None else acc + y
        for s in range(OUT_SLAB):
            o_ref[pl.ds(s, tm, stride=OUT_SLAB), :] = acc[:, s * LANES:(s + 1) * LANES]

    @pl.when(val_ref[i] == 0)
    def _():
        o_ref[...] = jnp.zeros_like(o_ref)


def _ffn(xs_sorted, tile_ea, tile_eb, tile_valid, w1, w3, w2, layer):
    n_tiles = xs_sorted.shape[0] // (MOE_TMG * IN_SLAB)
    up = lambda sel: pl.BlockSpec((None, None, D_MODEL, D_EXPERT),
                                  lambda i, ea, eb, v: (layer, sel(ea, eb)[i], 0, 0))
    down = lambda sel: pl.BlockSpec((None, None, D_EXPERT, D_MODEL),
                                    lambda i, ea, eb, v: (layer, sel(ea, eb)[i], 0, 0))
    first = lambda ea, eb: ea
    second = lambda ea, eb: eb
    return pl.pallas_call(
        _ffn_kernel,
        grid_spec=pltpu.PrefetchScalarGridSpec(
            num_scalar_prefetch=3,
            grid=(n_tiles,),
            in_specs=[
                pl.BlockSpec((MOE_TMG * IN_SLAB, LANES), lambda i, ea, eb, v: (i, 0)),
                up(first), up(first), down(first),
                up(second), up(second), down(second),
            ],
            out_specs=pl.BlockSpec((MOE_TMG * OUT_SLAB, LANES), lambda i, ea, eb, v: (i, 0)),
        ),
        out_shape=jax.ShapeDtypeStruct((n_tiles * MOE_TMG * OUT_SLAB, LANES), F32),
        name="moe_ffn",
        compiler_params=_cparams(("arbitrary",)),
    )(tile_ea, tile_eb, tile_valid, xs_sorted, w1, w3, w2, w1, w3, w2)


def _combine_kernel(pos_ref, x_ref, mod_ref, modc_ref, gfin_ref, ys_hbm, o_ref, buf_ref, sem_a, sem_b,
                    *, n_lat, final):
    tm = x_ref.shape[0]
    half = tm // 2
    halves = ((0, sem_a), (half, sem_b))

    for lo, sem in halves:
        def body(g, c, lo=lo, sem=sem):
            for u in range(DMA_UNROLL):
                r = lo + g * DMA_UNROLL + u
                _row_copy(_slab(ys_hbm, pos_ref[0, 0, r], OUT_SLAB), _slab(buf_ref, r, OUT_SLAB),
                          sem).start(priority=u % 2)
            return c

        lax.fori_loop(0, half // DMA_UNROLL, body, 0)

    for lo, sem in halves:
        part = buf_ref.at[pl.ds(lo * OUT_SLAB, half * OUT_SLAB), :]
        _row_copy(ys_hbm.at[pl.ds(0, half * OUT_SLAB), :], part, sem).wait()
        y = jnp.concatenate([buf_ref[pl.ds(lo * OUT_SLAB + s, half, stride=OUT_SLAB), :] for s in range(OUT_SLAB)],
                            axis=1)
        row = pl.program_id(1) * tm + lo + lax.broadcasted_iota(jnp.int32, (half, 1), 0)
        gate = jnp.where(row >= n_lat, modc_ref[5:6, :], mod_ref[5:6, :])
        out = x_ref[lo:lo + half, :] + gate * y
        if final:
            out = _rms(out) * gfin_ref[...]
        o_ref[lo:lo + half, :] = out


def _combine(xs, ys, pos, mod, gfin, n_lat, n_rows, tm, final):
    b, s_tot, _ = xs.shape
    n_t = n_rows // tm
    x_spec = pl.BlockSpec((None, tm, D_MODEL), lambda bi, t: (bi, t, 0))
    return pl.pallas_call(
        functools.partial(_combine_kernel, n_lat=n_lat, final=final),
        grid=(b, n_t),
        in_specs=[
            pl.BlockSpec((1, 1, tm), lambda bi, t: (bi * n_t + t, 0, 0), memory_space=pltpu.SMEM),
            x_spec,
            pl.BlockSpec((None, 6, D_MODEL), lambda bi, t: (bi, 0, 0)),
            pl.BlockSpec((None, 6, D_MODEL), lambda bi, t: (b, 0, 0)),
            pl.BlockSpec((1, D_MODEL), lambda bi, t: (0, 0)),
            pl.BlockSpec(memory_space=pl.ANY),
        ],
        out_specs=x_spec,
        out_shape=jax.ShapeDtypeStruct((b, n_rows, D_MODEL) if final else xs.shape, F32),
        scratch_shapes=[pltpu.VMEM((tm * OUT_SLAB, LANES), F32), pltpu.SemaphoreType.DMA(()),
                        pltpu.SemaphoreType.DMA(())],
        input_output_aliases={} if final else {1: 0},
        name="moe_combine",
        compiler_params=_cparams(("arbitrary", "arbitrary")),
    )(pos, xs, mod, mod, gfin, ys)


def _moe(xs, rows, info, counts, mod, w1, w3, w2, gfin, *, layer, n_lat, n_rows, final):
    b = xs.shape[0]
    n_tok = b * n_rows

    tmg = MOE_TMG
    cap = (n_tok // tmg + N_BUCKETS) * tmg
    cnt = counts[:N_BUCKETS, 0].astype(jnp.int32)
    padded = (cnt + tmg - 1) // tmg * tmg
    ends = jnp.cumsum(padded)
    starts = ends - padded
    bucket = info[:, 0, :].astype(jnp.int32)
    start_of = sum(jnp.where(bucket == k, starts[k], 0) for k in range(N_BUCKETS))
    pos = start_of + info[:, 1, :].astype(jnp.int32)
    tm_p = next(t for t in PERMUTE_TMS if n_tok % t == 0)
    tm_c = n_rows // COMBINE_TILES
    tile_start = jnp.arange(cap // tmg, dtype=jnp.int32) * tmg
    tile_valid = (tile_start < ends[-1]).astype(jnp.int32)
    tile_bkt = sum((tile_start >= ends[k]).astype(jnp.int32) for k in range(N_BUCKETS - 1))
    last_bkt = jnp.max(jnp.where(tile_valid > 0, tile_bkt, 0))
    tile_bkt = jnp.where(tile_valid > 0, tile_bkt, last_bkt)
    pair_lo = (0, 0, 0, 1, 1, 2)
    pair_hi = (1, 2, 3, 2, 3, 3)
    grp, pair = tile_bkt // PAIRS_PER_GROUP, tile_bkt % PAIRS_PER_GROUP
    tile_ea = grp * EXPERTS_PER_GROUP + sum(jnp.where(pair == p, pair_lo[p], 0) for p in range(PAIRS_PER_GROUP))
    tile_eb = grp * EXPERTS_PER_GROUP + sum(jnp.where(pair == p, pair_hi[p], 0) for p in range(PAIRS_PER_GROUP))

    xs_sorted = _permute(rows, pos.reshape(n_tok // tm_p, 1, tm_p), ends.astype(jnp.int32), cap, tm_p)
    ys = _ffn(xs_sorted, tile_ea.astype(jnp.int32), tile_eb.astype(jnp.int32), tile_valid, w1, w3, w2, layer)
    return _combine(xs, ys, pos.reshape(n_tok // tm_c, 1, tm_c), mod, gfin, n_lat, n_rows, tm_c, final)


def _rope_tables(n_lat, n_ctx):
    pos = jnp.arange(n_lat)
    row = (pos // GRID_W).astype(F32)
    col = (pos % GRID_W).astype(F32)
    n_freq = HEAD_DIM // 4
    inv = 1.0 / (ROPE_BASE ** (jnp.arange(n_freq, dtype=F32) / n_freq))
    lane = jnp.arange(LANES) % HEAD_DIM
    axis = lane // 32
    second = (lane % 32) // 16
    freq = inv[lane % 16]
    ang = jnp.where(axis[None, :] == 0, row[:, None], col[:, None]) * freq[None, :]
    cos = jnp.cos(ang)
    sin = jnp.sin(ang) * jnp.where(second == 0, -1.0, 1.0)[None, :]
    cos = jnp.concatenate([cos, jnp.ones((n_ctx, LANES), F32)], axis=0)
    sin = jnp.concatenate([sin, jnp.zeros((n_ctx, LANES), F32)], axis=0)
    q_scale = HEAD_DIM ** -0.5 * LOG2E
    return cos * q_scale, sin * q_scale, cos, sin


def _proj_cols():
    base = jnp.arange(D_IN)
    kb0 = base[2048:2112]
    kb1 = base[2112:2176]
    return jnp.concatenate([base[:2048], kb0, kb0, kb1, kb1, base[2176:]])


def kernel(x, c, ctx, c_ctx, w_ada, b_ada, norm_attn, norm_ffn, w_in, w_out, lambda_qk, subln, sink, w_router,
           b_router, w1, w3, w2, norm_final):
    b, n_lat, _ = x.shape
    n_ctx = ctx.shape[1]
    depth = w_in.shape[0]

    s_tot = n_lat + n_ctx
    c_all = jnp.zeros((MOD_ROWS, D_MODEL), F32).at[:b].set(c).at[b].set(c_ctx)
    mod_all = _ada(c_all, w_ada, b_ada).reshape(depth, MOD_ROWS, 6, D_MODEL)
    tables = _rope_tables(n_lat, n_ctx)
    cols = _proj_cols()

    wr_hi, wr_lo = _split_bf16(w_router.T)
    wr_t = jnp.concatenate([wr_hi, wr_lo], axis=0)
    br_col = b_router.reshape(N_EXPERTS, 1)
    gfin = norm_final.reshape(1, D_MODEL)

    w1b, w3b, w2b = w1.astype(BF16), w3.astype(BF16), w2.astype(BF16)
    ctx_t0 = n_lat // PROJ_TM

    lat_src, ctx_src, ctx_off = x, ctx, 0
    out = None
    for l in range(depth):
        last = l == depth - 1
        lam_init = 0.8 - 0.6 * math.exp(-0.3 * l)
        mod = mod_all[l]
        w_in_l = w_in[l][:, cols].astype(BF16)
        proj = _proj(lat_src, ctx_src, ctx_off, mod, norm_attn[l].reshape(1, D_MODEL), w_in_l, tables, n_lat, s_tot)
        oa = _attn_a(proj, lambda_qk[l], subln[l].reshape(2 * HEAD_DIM, 1), n_lat, not last, lam_init)
        ob = _attn_b(proj, sink[l], n_lat, not last)
        n_rows = n_lat if last else s_tot
        in_place = l > 0
        xs, rows, info, counts = _route(lat_src, ctx if in_place else ctx_src, 0 if in_place else ctx_off, in_place,
                                        oa, ob, w_out[l].astype(BF16), mod, norm_ffn[l].reshape(1, D_MODEL),
                                        wr_t, br_col, n_lat, n_rows, s_tot)
        res = _moe(xs, rows, info, counts, mod, w1b, w3b, w2b, gfin, layer=l, n_lat=n_lat, n_rows=n_rows,
                   final=last)
        if last:
            out = res
        else:
            lat_src, ctx_src, ctx_off = res, res, ctx_t0
    return out
```
